```python
import math
import jax, jax.numpy as jnp
from jax import lax
import numpy as np

D_MODEL = 2048
BATCH = 4
SEQ = 2048
DEPTH = 1

N_Q_HEADS = 32
N_KV_HEADS = 4
HEAD_DIM = 64
Q_PER_KV = N_Q_HEADS // N_KV_HEADS
WINDOW = 128
BLOCK = 128
ATTN_WIDTH = N_Q_HEADS * HEAD_DIM
KV_WIDTH = N_KV_HEADS * HEAD_DIM

CONV_WIDTH = D_MODEL
CONV_K = 3

N_BRANCHES = 2

Q_OFF = 0
K_OFF = Q_OFF + ATTN_WIDTH
V_OFF = K_OFF + KV_WIDTH
CB_OFF = V_OFF + KV_WIDTH
CC_OFF = CB_OFF + CONV_WIDTH
CH_OFF = CC_OFF + CONV_WIDTH
G_OFF = CH_OFF + CONV_WIDTH
IN_WIDTH = G_OFF + N_BRANCHES * D_MODEL

PEER_HEADS = 8
PEER_N_KEYS = 128
PEER_N_EXPERTS = PEER_N_KEYS * PEER_N_KEYS
PEER_TOPK = 16
PEER_QDIM = 256
PEER_HALF = PEER_QDIM // 2
PEER_CHUNK = 128

LN_EPS = 1e-5
ALPHA = (2.0 * DEPTH) ** 0.25
BETA = (8.0 * DEPTH) ** -0.25

kernel_name = "hybrid_swa_sink_shortconv_peer_deepnorm"


def layer_norm(x, g, b):
    xf = x.astype(jnp.float32)
    mu = jnp.mean(xf, axis=-1, keepdims=True)
    var = jnp.mean(jnp.square(xf - mu), axis=-1, keepdims=True)
    y = (xf - mu) * lax.rsqrt(var + LN_EPS) * g.astype(jnp.float32) + b.astype(jnp.float32)
    return y.astype(x.dtype)


def sliding_window_sink_attention(q, k, v, sinks):
    b_, s_ = q.shape[0], q.shape[1]
    nb = s_ // BLOCK
    qb = q.reshape(b_, nb, BLOCK, N_KV_HEADS, Q_PER_KV, HEAD_DIM)
    kb = k.reshape(b_, nb, BLOCK, N_KV_HEADS, HEAD_DIM)
    vb = v.reshape(b_, nb, BLOCK, N_KV_HEADS, HEAD_DIM)

    def with_prev(t):
        prev = jnp.pad(t, ((0, 0), (1, 0), (0, 0), (0, 0), (0, 0)))[:, :-1]
        return jnp.concatenate([prev, t], axis=2)

    kw = with_prev(kb)
    vw = with_prev(vb)
    scores = jnp.einsum('bnqhgd,bnkhd->bnhgqk', qb, kw).astype(jnp.float32) * (HEAD_DIM ** -0.5)

    qi = jnp.arange(BLOCK)[:, None]
    kj = jnp.arange(2 * BLOCK)[None, :]
    rel = BLOCK + qi - kj
    band = (rel >= 0) & (rel < WINDOW)
    kpos = jnp.arange(nb)[:, None] * BLOCK - BLOCK + kj
    mask = band[None] & (kpos >= 0)[:, None, :]
    scores = jnp.where(mask[None, :, None, None], scores, -jnp.inf)

    sink = sinks.astype(jnp.float32).reshape(N_KV_HEADS, Q_PER_KV)[None, None, :, :, None, None]
    m = jnp.maximum(jnp.max(scores, axis=-1, keepdims=True), sink)
    p = jnp.exp(scores - m)
    denom = jnp.sum(p, axis=-1, keepdims=True) + jnp.exp(sink - m)
    probs = (p / denom).astype(v.dtype)
    out = jnp.einsum('bnhgqk,bnkhd->bnqhgd', probs, vw)
    return out.reshape(b_, s_, ATTN_WIDTH)


def causal_short_conv(u, w):
    return lax.conv_general_dilated(
        u, w[:, None, :].astype(u.dtype), window_strides=(1,), padding=[(CONV_K - 1, 0)],
        dimension_numbers=('NWC', 'WIO', 'NWC'), feature_group_count=u.shape[-1])


def peer_ffn(x, w_q, sub_keys, u_tab, v_tab):
    b_, s_, d_ = x.shape
    t_ = b_ * s_
    xt = x.reshape(t_, d_)
    q = (xt @ w_q).reshape(t_, PEER_HEADS, 2, PEER_HALF)
    s = jnp.einsum('thpc,hpnc->thpn', q, sub_keys).astype(jnp.float32)
    top_s, top_i = lax.top_k(s, PEER_TOPK)
    cand_s = top_s[:, :, 0, :, None] + top_s[:, :, 1, None, :]
    cand_i = top_i[:, :, 0, :, None] * PEER_N_KEYS + top_i[:, :, 1, None, :]
    cand_s = cand_s.reshape(t_, PEER_HEADS, PEER_TOPK * PEER_TOPK)
    cand_i = cand_i.reshape(t_, PEER_HEADS, PEER_TOPK * PEER_TOPK)
    best_s, best_pos = lax.top_k(cand_s, PEER_TOPK)
    expert = jnp.take_along_axis(cand_i, best_pos, axis=-1)
    gate = jax.nn.softmax(best_s, axis=-1)

    nc = t_ // PEER_CHUNK

    def per_block(args):
        xc, ec, gc = args
        u = jnp.take(u_tab, ec, axis=0)
        act = jax.nn.gelu(jnp.einsum('cd,chkd->chk', xc, u).astype(jnp.float32), approximate=False)
        wgt = (gc * act).astype(xc.dtype)
        v = jnp.take(v_tab, ec, axis=0)
        return jnp.einsum('chk,chkd->cd', wgt, v)

    y = lax.map(per_block, (xt.reshape(nc, PEER_CHUNK, d_),
                            expert.reshape(nc, PEER_CHUNK, PEER_HEADS, PEER_TOPK),
                            gate.reshape(nc, PEER_CHUNK, PEER_HEADS, PEER_TOPK)))
    return y.reshape(b_, s_, d_)


def setup_inputs(seed: int = 0) -> dict:
    key = jax.random.key(seed)
    ks = jax.random.split(key, 17)
    f32 = jnp.float32
    nrm = lambda k, shape, scale: jax.random.normal(k, shape, f32) * scale
    x = jax.random.normal(ks[0], (BATCH, SEQ, D_MODEL), f32)
    w_in = nrm(ks[1], (DEPTH, D_MODEL, IN_WIDTH), D_MODEL ** -0.5)
    b_gate = nrm(ks[2], (DEPTH, N_BRANCHES * D_MODEL), 0.02)
    sinks = nrm(ks[3], (DEPTH, N_Q_HEADS), 0.5)
    conv_w = nrm(ks[4], (DEPTH, CONV_K, CONV_WIDTH), CONV_K ** -0.5)
    w_attn_out = nrm(ks[5], (DEPTH, ATTN_WIDTH, D_MODEL), ATTN_WIDTH ** -0.5)
    w_conv_out = nrm(ks[6], (DEPTH, CONV_WIDTH, D_MODEL), CONV_WIDTH ** -0.5)
    w_o = nrm(ks[7], (DEPTH, D_MODEL, D_MODEL), BETA * D_MODEL ** -0.5)
    ln1_g = 1.0 + nrm(ks[8], (DEPTH, D_MODEL), 0.02)
    ln1_b = nrm(ks[9], (DEPTH, D_MODEL), 0.02)
    peer_w_q = nrm(ks[10], (DEPTH, D_MODEL, PEER_HEADS * PEER_QDIM), D_MODEL ** -0.5)
    peer_sub_keys = nrm(ks[11], (DEPTH, PEER_HEADS, 2, PEER_N_KEYS, PEER_HALF), PEER_HALF ** -0.5)
    peer_u = nrm(ks[12], (DEPTH, PEER_N_EXPERTS, D_MODEL), D_MODEL ** -0.5)
    peer_v = nrm(ks[13], (DEPTH, PEER_N_EXPERTS, D_MODEL), BETA)
    ln2_g = 1.0 + nrm(ks[14], (DEPTH, D_MODEL), 0.02)
    ln2_b = nrm(ks[15], (DEPTH, D_MODEL), 0.02)
    return {"x": x, "w_in": w_in, "b_gate": b_gate, "sinks": sinks, "conv_w": conv_w,
            "w_attn_out": w_attn_out, "w_conv_out": w_conv_out, "w_o": w_o,
            "ln1_g": ln1_g, "ln1_b": ln1_b, "peer_w_q": peer_w_q, "peer_sub_keys": peer_sub_keys,
            "peer_u": peer_u, "peer_v": peer_v, "ln2_g": ln2_g, "ln2_b": ln2_b}


def reference(x, w_in, b_gate, sinks, conv_w, w_attn_out, w_conv_out, w_o, ln1_g, ln1_b,
              peer_w_q, peer_sub_keys, peer_u, peer_v, ln2_g, ln2_b):
    b_, s_ = x.shape[0], x.shape[1]
    h = x
    for l in range(DEPTH):
        z = h @ w_in[l]
        q = z[..., Q_OFF:K_OFF].reshape(b_, s_, N_Q_HEADS, HEAD_DIM)
        k = z[..., K_OFF:V_OFF].reshape(b_, s_, N_KV_HEADS, HEAD_DIM)
        v = z[..., V_OFF:CB_OFF].reshape(b_, s_, N_KV_HEADS, HEAD_DIM)
        cb = z[..., CB_OFF:CC_OFF]
        cc = z[..., CC_OFF:CH_OFF]
        ch = z[..., CH_OFF:G_OFF]
        gates = jax.nn.sigmoid((z[..., G_OFF:] + b_gate[l]).astype(jnp.float32)).astype(h.dtype)

        attn = sliding_window_sink_attention(q, k, v, sinks[l]) @ w_attn_out[l]
        conv = (cb * causal_short_conv(cc * ch, conv_w[l])) @ w_conv_out[l]
        merged = gates[..., :D_MODEL] * attn + gates[..., D_MODEL:] * conv
        mixed = merged @ w_o[l]
        h = layer_norm(ALPHA * h + mixed, ln1_g[l], ln1_b[l])

        ffn = peer_ffn(h, peer_w_q[l], peer_sub_keys[l], peer_u[l], peer_v[l])
        h = layer_norm(ALPHA * h + ffn, ln2_g[l], ln2_b[l])
    return h
```

```python
import functools
import math

import jax
import jax.numpy as jnp
from jax import lax
from jax.experimental import pallas as pl
from jax.experimental.pallas import tpu as pltpu

F32 = jnp.float32
BF16 = jnp.bfloat16

D_MODEL = 2048
N_Q_HEADS = 32
N_KV_HEADS = 4
HEAD_DIM = 64
Q_PER_KV = N_Q_HEADS // N_KV_HEADS
WINDOW = 128
ATTN_WIDTH = N_Q_HEADS * HEAD_DIM
KV_WIDTH = N_KV_HEADS * HEAD_DIM
QKV_WIDTH = ATTN_WIDTH + 2 * KV_WIDTH
CONV_WIDTH = D_MODEL
CONV_K = 3
K_OFF = ATTN_WIDTH
V_OFF = K_OFF + KV_WIDTH
CB_OFF = V_OFF + KV_WIDTH
CC_OFF = CB_OFF + CONV_WIDTH
CH_OFF = CC_OFF + CONV_WIDTH
G_OFF = CH_OFF + CONV_WIDTH

PEER_HEADS = 8
PEER_N_KEYS = 128
PEER_N_EXPERTS = PEER_N_KEYS * PEER_N_KEYS
PEER_TOPK = 16
PEER_HALF = 128
PEER_QDIM = 2 * PEER_HALF

LN_EPS = 1e-5
DEPTH = 1
ALPHA = (2.0 * DEPTH) ** 0.25

V7X_LANES = 128
V7X_SUBLANES = 8
V7X_VMEM_LIMIT_BYTES = 56 * 1024 * 1024

N_RANKS = PEER_TOPK + 1
_CAND = [(r1, r2) for r1 in range(N_RANKS) for r2 in range(N_RANKS) if (r1 + 1) * (r2 + 1) <= N_RANKS]
N_CAND_ROWS = -(-len(_CAND) // V7X_SUBLANES) * V7X_SUBLANES


def _tiles(n_tokens, seq):
    return dict(
        proj=min(1024, seq),
        mix=min(512, n_tokens),
        route=min(256, n_tokens),
        peer=min(512, n_tokens),
        col=512,
        mix_col=256,
        experts=512,
    )


def _params(*sem):
    return pltpu.CompilerParams(dimension_semantics=sem, vmem_limit_bytes=V7X_VMEM_LIMIT_BYTES)


def _dot(a, b):
    return jnp.dot(a, b, preferred_element_type=F32)


def _matmul_kernel(x_ref, w_ref, o_ref):
    o_ref[...] = _dot(x_ref[...], w_ref[...]).astype(o_ref.dtype)


def _qkv_proj(x_bf, w_in_bf, tm, tn):
    n_tok = x_bf.shape[0]
    return pl.pallas_call(
        _matmul_kernel,
        grid=(QKV_WIDTH // tn, n_tok // tm),
        in_specs=[pl.BlockSpec((tm, D_MODEL), lambda j, i: (i, 0)),
                  pl.BlockSpec((D_MODEL, tn), lambda j, i: (0, j))],
        out_specs=pl.BlockSpec((tm, tn), lambda j, i: (i, j)),
        out_shape=jax.ShapeDtypeStruct((n_tok, QKV_WIDTH), BF16),
        compiler_params=_params("arbitrary", "arbitrary"),
        name="qkv_proj",
    )(x_bf, w_in_bf)


def _conv_kernel(x_ref, wcb_ref, wcc_ref, wch_ref, cw_ref, o_ref, carry_ref, *, tiles_per_seq):
    i = pl.program_id(1)
    x = x_ref[...]
    cb = _dot(x, wcb_ref[...])
    u = _dot(x, wcc_ref[...]) * _dot(x, wch_ref[...])
    tm = u.shape[0]

    @pl.when(i % tiles_per_seq == 0)
    def _():
        carry_ref[...] = jnp.zeros_like(carry_ref)

    prev = carry_ref[...]
    row = lax.broadcasted_iota(jnp.int32, u.shape, 0)
    u1 = jnp.where(row == 0, prev[7:8], pltpu.roll(u, 1, 0))
    u2 = pltpu.roll(u, 2, 0)
    u2 = jnp.where(row == 0, prev[6:7], jnp.where(row == 1, prev[7:8], u2))
    cw = cw_ref[...]
    y = cb * (cw[0:1] * u2 + cw[1:2] * u1 + cw[2:3] * u)
    o_ref[...] = y.astype(o_ref.dtype)
    carry_ref[...] = u[tm - V7X_SUBLANES:tm]


def _conv_branch(x_bf, w_in_bf, conv_w, seq, tm, nc):
    n_tok = x_bf.shape[0]
    wspec = lambda off: pl.BlockSpec((D_MODEL, nc), lambda j, i, o=off // nc: (0, o + j))
    return pl.pallas_call(
        functools.partial(_conv_kernel, tiles_per_seq=seq // tm),
        grid=(CONV_WIDTH // nc, n_tok // tm),
        in_specs=[pl.BlockSpec((tm, D_MODEL), lambda j, i: (i, 0)),
                  wspec(CB_OFF), wspec(CC_OFF), wspec(CH_OFF),
                  pl.BlockSpec((CONV_K, nc), lambda j, i: (0, j))],
        out_specs=pl.BlockSpec((tm, nc), lambda j, i: (i, j)),
        out_shape=jax.ShapeDtypeStruct((n_tok, CONV_WIDTH), BF16),
        scratch_shapes=[pltpu.VMEM((V7X_SUBLANES, nc), F32)],
        compiler_params=_params("arbitrary", "arbitrary"),
        name="conv_branch",
    )(x_bf, w_in_bf, w_in_bf, w_in_bf, conv_w)


def _attn_kernel(sinks_ref, q_ref, kc_ref, kp_ref, vc_ref, vp_ref, o_ref):
    n = pl.program_id(1)
    rows = Q_PER_KV * WINDOW
    qi = lax.broadcasted_iota(jnp.int32, (rows, 2 * WINDOW), 0) & (WINDOW - 1)
    kj = lax.broadcasted_iota(jnp.int32, (rows, 2 * WINDOW), 1)
    rel = WINDOW + qi - kj
    mask = (rel >= 0) & (rel < WINDOW) & ((kj >= WINDOW) | (n > 0))
    for j in range(N_KV_HEADS):
        heads = [j * Q_PER_KV + g for g in range(Q_PER_KV)]
        qg = jnp.concatenate([q_ref[:, h * HEAD_DIM:(h + 1) * HEAD_DIM] for h in heads], axis=0)
        ksl = slice(j * HEAD_DIM, (j + 1) * HEAD_DIM)
        kw = jnp.concatenate([kp_ref[:, ksl], kc_ref[:, ksl]], axis=0)
        vw = jnp.concatenate([vp_ref[:, ksl], vc_ref[:, ksl]], axis=0)
        s = lax.dot_general(qg, kw, (((1,), (1,)), ((), ())), preferred_element_type=F32)
        s = jnp.where(mask, s * (HEAD_DIM ** -0.5), -jnp.inf)
        sink = jnp.concatenate([jnp.full((WINDOW, 1), sinks_ref[h], F32) for h in heads], axis=0)
        m = jnp.maximum(jnp.max(s, axis=-1, keepdims=True), sink)
        p = jnp.exp(s - m)
        denom = jnp.sum(p, axis=-1, keepdims=True) + jnp.exp(sink - m)
        probs = (p / denom).astype(BF16)
        o = _dot(probs, vw)
        for g, h in enumerate(heads):
            o_ref[:, h * HEAD_DIM:(h + 1) * HEAD_DIM] = o[g * WINDOW:(g + 1) * WINDOW].astype(o_ref.dtype)


def _attention(qkv, sinks, batch, seq):
    n_tok = qkv.shape[0]
    nb = seq // WINDOW
    cur = lambda b, n: b * nb + n
    prev = lambda b, n: b * nb + jnp.maximum(n - 1, 0)
    kcol, vcol = K_OFF // KV_WIDTH, V_OFF // KV_WIDTH
    return pl.pallas_call(
        _attn_kernel,
        grid=(batch, nb),
        in_specs=[pl.BlockSpec(memory_space=pltpu.SMEM),
                  pl.BlockSpec((WINDOW, ATTN_WIDTH), lambda b, n: (cur(b, n), 0)),
                  pl.BlockSpec((WINDOW, KV_WIDTH), lambda b, n: (cur(b, n), kcol)),
                  pl.BlockSpec((WINDOW, KV_WIDTH), lambda b, n: (prev(b, n), kcol)),
                  pl.BlockSpec((WINDOW, KV_WIDTH), lambda b, n: (cur(b, n), vcol)),
                  pl.BlockSpec((WINDOW, KV_WIDTH), lambda b, n: (prev(b, n), vcol))],
        out_specs=pl.BlockSpec((WINDOW, ATTN_WIDTH), lambda b, n: (cur(b, n), 0)),
        out_shape=jax.ShapeDtypeStruct((n_tok, ATTN_WIDTH), BF16),
        compiler_params=_params("arbitrary", "arbitrary"),
        name="swa_attention",
    )(sinks, qkv, qkv, qkv, qkv, qkv)


def _layer_norm(r, g, b):
    mu = jnp.mean(r, axis=-1, keepdims=True)
    d = r - mu
    var = jnp.mean(d * d, axis=-1, keepdims=True)
    return d * lax.rsqrt(var + LN_EPS) * g + b


def _sigmoid(z):
    return 1.0 / (1.0 + jnp.exp(-z))


def _mix_kernel(xb_ref, xf_ref, a_ref, c_ref, wga_ref, wgc_ref, bga_ref, bgc_ref, wao_ref, wco_ref,
                wo_ref, g_ref, b_ref, hT_ref, acc_ref):
    j = pl.program_id(1)
    xb = xb_ref[...]
    ga = _sigmoid(_dot(xb, wga_ref[...]) + bga_ref[...])
    gc = _sigmoid(_dot(xb, wgc_ref[...]) + bgc_ref[...])
    merged = ga * _dot(a_ref[...], wao_ref[...]) + gc * _dot(c_ref[...], wco_ref[...])
    part = _dot(merged.astype(BF16), wo_ref[...])

    @pl.when(j == 0)
    def _():
        acc_ref[...] = part

    @pl.when(j > 0)
    def _():
        acc_ref[...] += part

    @pl.when(j == pl.num_programs(1) - 1)
    def _():
        h1 = _layer_norm(ALPHA * xf_ref[...] + acc_ref[...], g_ref[...], b_ref[...])
        hT_ref[...] = h1.T


def _mix_ln1(x_bf, x_f32, attn_o, conv_o, w_in_bf, b_gate, w_ao, w_co, w_o, ln_g, ln_b, tm, nc):
    n_tok = x_bf.shape[0]
    row = lambda i, j: (i, 0)
    gcol = G_OFF // nc
    return pl.pallas_call(
        _mix_kernel,
        grid=(n_tok // tm, D_MODEL // nc),
        in_specs=[pl.BlockSpec((tm, D_MODEL), row), pl.BlockSpec((tm, D_MODEL), row),
                  pl.BlockSpec((tm, ATTN_WIDTH), row), pl.BlockSpec((tm, CONV_WIDTH), row),
                  pl.BlockSpec((D_MODEL, nc), lambda i, j: (0, gcol + j)),
                  pl.BlockSpec((D_MODEL, nc), lambda i, j: (0, gcol + D_MODEL // nc + j)),
                  pl.BlockSpec((1, nc), lambda i, j: (0, j)),
                  pl.BlockSpec((1, nc), lambda i, j: (0, D_MODEL // nc + j)),
                  pl.BlockSpec((ATTN_WIDTH, nc), lambda i, j: (0, j)),
                  pl.BlockSpec((CONV_WIDTH, nc), lambda i, j: (0, j)),
                  pl.BlockSpec((nc, D_MODEL), lambda i, j: (j, 0)),
                  pl.BlockSpec((1, D_MODEL), lambda i, j: (0, 0)),
                  pl.BlockSpec((1, D_MODEL), lambda i, j: (0, 0))],
        out_specs=pl.BlockSpec((D_MODEL, tm), lambda i, j: (0, i)),
        out_shape=jax.ShapeDtypeStruct((D_MODEL, n_tok), F32),
        scratch_shapes=[pltpu.VMEM((tm, D_MODEL), F32)],
        compiler_params=_params("arbitrary", "arbitrary"),
        name="mix_ln1",
    )(x_bf, x_f32, attn_o, conv_o, w_in_bf, w_in_bf, b_gate, b_gate, w_ao, w_co, w_o, ln_g, ln_b)


def _top_values(vals, k):
    out = []
    for _ in range(k):
        m = jnp.max(vals, axis=0, keepdims=True)
        out.append(m)
        vals = jnp.where(vals == m, -jnp.inf, vals)
    return out


def _route_kernel(hT_ref, wqT_ref, sk_ref, r1_ref, r2_ref, t1_ref, e1_ref, s2_ref, e2_ref, qT_ref):
    qT_ref[...] = _dot(wqT_ref[...], hT_ref[...].astype(BF16))
    r1t = r1_ref[...]
    r2t = r2_ref[...]

    def head(h, carry):
        base = pl.multiple_of(h * PEER_QDIM, PEER_QDIM)
        q1 = qT_ref[pl.ds(base, PEER_HALF), :].astype(BF16)
        q2 = qT_ref[pl.ds(base + PEER_HALF, PEER_HALF), :].astype(BF16)
        s1 = _dot(sk_ref[h, 0], q1)
        s2 = _dot(sk_ref[h, 1], q2)
        a = _top_values(s1, N_RANKS)
        b = _top_values(s2, N_RANKS)
        ca = jnp.full(r1t.shape, -jnp.inf, F32)
        cb = jnp.zeros(r2t.shape, F32)
        for r in range(N_RANKS):
            ca = jnp.where(r1t == r, a[r], ca)
            cb = jnp.where(r2t == r, b[r], cb)
        csum = ca + cb
        cprod = jnp.exp(ca - a[0]) * jnp.exp(cb - b[0])
        c = _top_values(csum, N_RANKS)
        thr = 0.5 * (c[PEER_TOPK - 1] + c[PEER_TOPK])
        denom = jnp.sum(jnp.where(csum >= c[PEER_TOPK - 1], cprod, 0.0), axis=0, keepdims=True)
        t1_ref[h] = thr - s1
        e1_ref[h] = jnp.exp(s1 - a[0]) * (1.0 / denom)
        s2_ref[h] = s2
        e2_ref[h] = jnp.exp(s2 - b[0])
        return carry

    lax.fori_loop(0, PEER_HEADS, head, 0)


def _peer_route(hT, wqT_bf, sub_keys_bf, tm):
    n_tok = hT.shape[1]
    r1 = [[p[0]] * tm for p in _CAND] + [[-1] * tm] * (N_CAND_ROWS - len(_CAND))
    r2 = [[p[1]] * tm for p in _CAND] + [[-1] * tm] * (N_CAND_ROWS - len(_CAND))
    r1 = jnp.asarray(r1, jnp.int32)
    r2 = jnp.asarray(r2, jnp.int32)
    out = jax.ShapeDtypeStruct((PEER_HEADS, PEER_N_KEYS, n_tok), F32)
    ospec = pl.BlockSpec((PEER_HEADS, PEER_N_KEYS, tm), lambda i: (0, 0, i))
    const2 = lambda i: (0, 0)
    return pl.pallas_call(
        _route_kernel,
        grid=(n_tok // tm,),
        in_specs=[pl.BlockSpec((D_MODEL, tm), lambda i: (0, i)),
                  pl.BlockSpec((PEER_HEADS * PEER_QDIM, D_MODEL), const2),
                  pl.BlockSpec((PEER_HEADS, 2, PEER_N_KEYS, PEER_HALF), lambda i: (0, 0, 0, 0)),
                  pl.BlockSpec((N_CAND_ROWS, tm), const2),
                  pl.BlockSpec((N_CAND_ROWS, tm), const2)],
        out_specs=[ospec] * 4,
        out_shape=[out] * 4,
        scratch_shapes=[pltpu.VMEM((PEER_HEADS * PEER_QDIM, tm), F32)],
        compiler_params=_params("arbitrary"),
        name="peer_route",
    )(hT, wqT_bf, sub_keys_bf, r1, r2)


def _gelu(x):
    return 0.5 * x * (1.0 + lax.erf(x * (2.0 ** -0.5)))


def _peer_kernel(hT_ref, u_ref, vT_ref, t1_ref, e1_ref, s2_ref, e2_ref, g_ref, b_ref, o_ref,
                 hb_ref, acc_ref):
    j = pl.program_id(1)

    @pl.when(j == 0)
    def _():
        hb_ref[...] = hT_ref[...].astype(BF16)

    act = _gelu(_dot(u_ref[...], hb_ref[...]))
    slabs = []
    for i1 in range(act.shape[0] // PEER_N_KEYS):
        w = None
        for h in range(PEER_HEADS):
            sel = s2_ref[h] >= t1_ref[h, 0, i1:i1 + 1, :]
            wh = jnp.where(sel, e1_ref[h, 0, i1:i1 + 1, :] * e2_ref[h], 0.0)
            w = wh if w is None else w + wh
        slabs.append((act[i1 * PEER_N_KEYS:(i1 + 1) * PEER_N_KEYS] * w).astype(BF16))
    part = _dot(vT_ref[...], jnp.concatenate(slabs, axis=0))

    @pl.when(j == 0)
    def _():
        acc_ref[...] = part

    @pl.when(j > 0)
    def _():
        acc_ref[...] += part

    @pl.when(j == pl.num_programs(1) - 1)
    def _():
        r = (ALPHA * hT_ref[...] + acc_ref[...]).T
        o_ref[...] = _layer_norm(r, g_ref[...], b_ref[...])


def _peer_experts(hT, u_bf, vT_bf, t1, e1, s2, e2, ln_g, ln_b, tm, te):
    n_tok = hT.shape[1]
    g = te // PEER_N_KEYS
    split = lambda a: a.reshape(PEER_HEADS, PEER_N_KEYS // g, g, n_tok)
    tok = lambda i, j: (0, 0, i)
    blk = lambda i, j: (0, j, 0, i)
    return pl.pallas_call(
        _peer_kernel,
        grid=(n_tok // tm, PEER_N_EXPERTS // te),
        in_specs=[pl.BlockSpec((D_MODEL, tm), lambda i, j: (0, i)),
                  pl.BlockSpec((te, D_MODEL), lambda i, j: (j, 0)),
                  pl.BlockSpec((D_MODEL, te), lambda i, j: (0, j)),
                  pl.BlockSpec((PEER_HEADS, 1, g, tm), blk),
                  pl.BlockSpec((PEER_HEADS, 1, g, tm), blk),
                  pl.BlockSpec((PEER_HEADS, PEER_N_KEYS, tm), tok),
                  pl.BlockSpec((PEER_HEADS, PEER_N_KEYS, tm), tok),
                  pl.BlockSpec((1, D_MODEL), lambda i, j: (0, 0)),
                  pl.BlockSpec((1, D_MODEL), lambda i, j: (0, 0))],
        out_specs=pl.BlockSpec((tm, D_MODEL), lambda i, j: (i, 0)),
        out_shape=jax.ShapeDtypeStruct((n_tok, D_MODEL), F32),
        scratch_shapes=[pltpu.VMEM((D_MODEL, tm), BF16), pltpu.VMEM((D_MODEL, tm), F32)],
        compiler_params=_params("arbitrary", "arbitrary"),
        name="peer_experts",
    )(hT, u_bf, vT_bf, split(t1), split(e1), s2, e2, ln_g, ln_b)


def _layer(h, batch, seq, w_in, b_gate, sinks, conv_w, w_ao, w_co, w_o, ln1_g, ln1_b,
           w_q, sub_keys, u_tab, v_tab, ln2_g, ln2_b):
    n_tok = h.shape[0]
    t = _tiles(n_tok, seq)
    row = lambda v: v.reshape(1, -1)
    h_bf = h.astype(BF16)
    w_in_bf = w_in.astype(BF16)
    qkv = _qkv_proj(h_bf, w_in_bf, t["proj"], t["col"])
    attn_o = _attention(qkv, sinks, batch, seq)
    conv_o = _conv_branch(h_bf, w_in_bf, conv_w, seq, t["proj"], t["col"])
    hT = _mix_ln1(h_bf, h, attn_o, conv_o, w_in_bf, row(b_gate), w_ao.astype(BF16), w_co.astype(BF16),
                  w_o.astype(BF16), row(ln1_g), row(ln1_b), t["mix"], t["mix_col"])
    t1, e1, s2, e2 = _peer_route(hT, w_q.T.astype(BF16), sub_keys.astype(BF16), t["route"])
    return _peer_experts(hT, u_tab.astype(BF16), v_tab.T.astype(BF16), t1, e1, s2, e2,
                         row(ln2_g), row(ln2_b), t["peer"], t["experts"])


def kernel(x, w_in, b_gate, sinks, conv_w, w_attn_out, w_conv_out, w_o, ln1_g, ln1_b, peer_w_q,
           peer_sub_keys, peer_u, peer_v, ln2_g, ln2_b):
    batch, seq, d_model = x.shape
    assert d_model == D_MODEL and seq % WINDOW == 0
    h = x.reshape(batch * seq, d_model)
    for l in range(w_in.shape[0]):
        h = _layer(h, batch, seq, w_in[l], b_gate[l], sinks[l], conv_w[l], w_attn_out[l],
                   w_conv_out[l], w_o[l], ln1_g[l], ln1_b[l], peer_w_q[l], peer_sub_keys[l],
                   peer_u[l], peer_v[l], ln2_g[l], ln2_b[l])
    return h.reshape(batch, seq, d_model)
```

```python
import functools
import math

import jax
import jax.numpy as jnp
from jax import lax
from jax.experimental import pallas as pl
from jax.experimental.pallas import tpu as pltpu

F32 = jnp.float32
BF16 = jnp.bfloat16

D_MODEL = 2048
N_Q_HEADS = 32
N_KV_HEADS = 4
HEAD_DIM = 64
Q_PER_KV = N_Q_HEADS // N_KV_HEADS
WINDOW = 128
ATTN_WIDTH = N_Q_HEADS * HEAD_DIM
KV_WIDTH = N_KV_HEADS * HEAD_DIM
QKV_WIDTH = ATTN_WIDTH + 2 * KV_WIDTH
CONV_WIDTH = D_MODEL
CONV_K = 3
K_OFF = ATTN_WIDTH
V_OFF = K_OFF + KV_WIDTH
CB_OFF = V_OFF + KV_WIDTH
CC_OFF = CB_OFF + CONV_WIDTH
CH_OFF = CC_OFF + CONV_WIDTH
G_OFF = CH_OFF + CONV_WIDTH

PEER_HEADS = 8
PEER_N_KEYS = 128
PEER_N_EXPERTS = PEER_N_KEYS * PEER_N_KEYS
PEER_TOPK = 16
PEER_HALF = 128
PEER_QDIM = 2 * PEER_HALF

LN_EPS = 1e-5
DEPTH = 1
ALPHA = (2.0 * DEPTH) ** 0.25

V7X_LANES = 128
V7X_SUBLANES = 8
V7X_VMEM_LIMIT_BYTES = 56 * 1024 * 1024

N_RANKS = PEER_TOPK + 1
_CAND = [(r1, r2) for r1 in range(N_RANKS) for r2 in range(N_RANKS) if (r1 + 1) * (r2 + 1) <= N_RANKS]
N_CAND_ROWS = -(-len(_CAND) // V7X_SUBLANES) * V7X_SUBLANES


def _tiles(n_tokens, seq):
    return dict(
        proj=min(1024, seq),
        mix=min(512, n_tokens),
        route=min(256, n_tokens),
        peer=min(512, n_tokens),
        col=512,
        mix_col=256,
        experts=1024,
    )


def _params(*sem):
    return pltpu.CompilerParams(dimension_semantics=sem, vmem_limit_bytes=V7X_VMEM_LIMIT_BYTES)


def _dot(a, b):
    return jnp.dot(a, b, preferred_element_type=F32)


def _matmul_kernel(x_ref, w_ref, o_ref):
    o_ref[...] = _dot(x_ref[...], w_ref[...]).astype(o_ref.dtype)


def _qkv_proj(x_bf, w_in_bf, tm, tn):
    n_tok = x_bf.shape[0]
    return pl.pallas_call(
        _matmul_kernel,
        grid=(QKV_WIDTH // tn, n_tok // tm),
        in_specs=[pl.BlockSpec((tm, D_MODEL), lambda j, i: (i, 0)),
                  pl.BlockSpec((D_MODEL, tn), lambda j, i: (0, j))],
        out_specs=pl.BlockSpec((tm, tn), lambda j, i: (i, j)),
        out_shape=jax.ShapeDtypeStruct((n_tok, QKV_WIDTH), BF16),
        compiler_params=_params("arbitrary", "arbitrary"),
        name="qkv_proj",
    )(x_bf, w_in_bf)


def _conv_kernel(x_ref, wcb_ref, wcc_ref, wch_ref, cw_ref, o_ref, carry_ref, *, tiles_per_seq):
    i = pl.program_id(1)
    x = x_ref[...]
    cb = _dot(x, wcb_ref[...])
    u = _dot(x, wcc_ref[...]) * _dot(x, wch_ref[...])
    tm = u.shape[0]

    @pl.when(i % tiles_per_seq == 0)
    def _():
        carry_ref[...] = jnp.zeros_like(carry_ref)

    prev = carry_ref[...]
    row = lax.broadcasted_iota(jnp.int32, u.shape, 0)
    u1 = jnp.where(row == 0, prev[7:8], pltpu.roll(u, 1, 0))
    u2 = pltpu.roll(u, 2, 0)
    u2 = jnp.where(row == 0, prev[6:7], jnp.where(row == 1, prev[7:8], u2))
    cw = cw_ref[...]
    y = cb * (cw[0:1] * u2 + cw[1:2] * u1 + cw[2:3] * u)
    o_ref[...] = y.astype(o_ref.dtype)
    carry_ref[...] = u[tm - V7X_SUBLANES:tm]


def _conv_branch(x_bf, w_in_bf, conv_w, seq, tm, nc):
    n_tok = x_bf.shape[0]
    wspec = lambda off: pl.BlockSpec((D_MODEL, nc), lambda j, i, o=off // nc: (0, o + j))
    return pl.pallas_call(
        functools.partial(_conv_kernel, tiles_per_seq=seq // tm),
        grid=(CONV_WIDTH // nc, n_tok // tm),
        in_specs=[pl.BlockSpec((tm, D_MODEL), lambda j, i: (i, 0)),
                  wspec(CB_OFF), wspec(CC_OFF), wspec(CH_OFF),
                  pl.BlockSpec((CONV_K, nc), lambda j, i: (0, j))],
        out_specs=pl.BlockSpec((tm, nc), lambda j, i: (i, j)),
        out_shape=jax.ShapeDtypeStruct((n_tok, CONV_WIDTH), BF16),
        scratch_shapes=[pltpu.VMEM((V7X_SUBLANES, nc), F32)],
        compiler_params=_params("arbitrary", "arbitrary"),
        name="conv_branch",
    )(x_bf, w_in_bf, w_in_bf, w_in_bf, conv_w)


def _attn_kernel(sinks_ref, q_ref, kc_ref, kp_ref, vc_ref, vp_ref, o_ref):
    n = pl.program_id(1)
    cols = Q_PER_KV * WINDOW
    kj = lax.broadcasted_iota(jnp.int32, (2 * WINDOW, cols), 0)
    qi = lax.broadcasted_iota(jnp.int32, (2 * WINDOW, cols), 1) & (WINDOW - 1)
    rel = WINDOW + qi - kj
    mask = (rel >= 0) & (rel < WINDOW) & ((kj >= WINDOW) | (n > 0))
    scale = HEAD_DIM ** -0.5
    for j in range(N_KV_HEADS):
        heads = [j * Q_PER_KV + g for g in range(Q_PER_KV)]
        qg = jnp.concatenate([q_ref[:, h * HEAD_DIM:(h + 1) * HEAD_DIM] for h in heads], axis=0)
        ksl = slice(j * HEAD_DIM, (j + 1) * HEAD_DIM)
        kw = jnp.concatenate([kp_ref[:, ksl], kc_ref[:, ksl]], axis=0)
        vw = jnp.concatenate([vp_ref[:, ksl], vc_ref[:, ksl]], axis=0)
        s = lax.dot_general(kw, qg * scale, (((1,), (1,)), ((), ())), preferred_element_type=F32)
        s = jnp.where(mask, s, -jnp.inf)
        sink = jnp.concatenate([jnp.full((1, WINDOW), sinks_ref[h], F32) for h in heads], axis=1)
        m = jnp.maximum(jnp.max(s, axis=0, keepdims=True), sink)
        p = jnp.exp(s - m)
        denom = jnp.sum(p, axis=0, keepdims=True) + jnp.exp(sink - m)
        probs = (p / denom).astype(BF16)
        o = lax.dot_general(probs, vw, (((0,), (0,)), ((), ())), preferred_element_type=F32)
        for g, h in enumerate(heads):
            o_ref[:, h * HEAD_DIM:(h + 1) * HEAD_DIM] = o[g * WINDOW:(g + 1) * WINDOW].astype(o_ref.dtype)


def _attention(qkv, sinks, batch, seq):
    n_tok = qkv.shape[0]
    nb = seq // WINDOW
    cur = lambda b, n: b * nb + n
    prev = lambda b, n: b * nb + jnp.maximum(n - 1, 0)
    kcol, vcol = K_OFF // KV_WIDTH, V_OFF // KV_WIDTH
    return pl.pallas_call(
        _attn_kernel,
        grid=(batch, nb),
        in_specs=[pl.BlockSpec(memory_space=pltpu.SMEM),
                  pl.BlockSpec((WINDOW, ATTN_WIDTH), lambda b, n: (cur(b, n), 0)),
                  pl.BlockSpec((WINDOW, KV_WIDTH), lambda b, n: (cur(b, n), kcol)),
                  pl.BlockSpec((WINDOW, KV_WIDTH), lambda b, n: (prev(b, n), kcol)),
                  pl.BlockSpec((WINDOW, KV_WIDTH), lambda b, n: (cur(b, n), vcol)),
                  pl.BlockSpec((WINDOW, KV_WIDTH), lambda b, n: (prev(b, n), vcol))],
        out_specs=pl.BlockSpec((WINDOW, ATTN_WIDTH), lambda b, n: (cur(b, n), 0)),
        out_shape=jax.ShapeDtypeStruct((n_tok, ATTN_WIDTH), BF16),
        compiler_params=_params("arbitrary", "arbitrary"),
        name="swa_attention",
    )(sinks, qkv, qkv, qkv, qkv, qkv)


def _layer_norm(r, g, b):
    mu = jnp.mean(r, axis=-1, keepdims=True)
    d = r - mu
    var = jnp.mean(d * d, axis=-1, keepdims=True)
    return d * lax.rsqrt(var + LN_EPS) * g + b


def _sigmoid(z):
    return 1.0 / (1.0 + jnp.exp(-z))


def _mix_kernel(xb_ref, xf_ref, a_ref, c_ref, wga_ref, wgc_ref, bga_ref, bgc_ref, wao_ref, wco_ref,
                wo_ref, g_ref, b_ref, hT_ref, acc_ref):
    j = pl.program_id(1)
    xb = xb_ref[...]
    ga = _sigmoid(_dot(xb, wga_ref[...]) + bga_ref[...])
    gc = _sigmoid(_dot(xb, wgc_ref[...]) + bgc_ref[...])
    merged = ga * _dot(a_ref[...], wao_ref[...]) + gc * _dot(c_ref[...], wco_ref[...])
    part = _dot(merged.astype(BF16), wo_ref[...])

    @pl.when(j == 0)
    def _():
        acc_ref[...] = part

    @pl.when(j > 0)
    def _():
        acc_ref[...] += part

    @pl.when(j == pl.num_programs(1) - 1)
    def _():
        h1 = _layer_norm(ALPHA * xf_ref[...] + acc_ref[...], g_ref[...], b_ref[...])
        hT_ref[...] = h1.T


def _mix_ln1(x_bf, x_f32, attn_o, conv_o, w_in_bf, b_gate, w_ao, w_co, w_o, ln_g, ln_b, tm, nc):
    n_tok = x_bf.shape[0]
    row = lambda i, j: (i, 0)
    gcol = G_OFF // nc
    return pl.pallas_call(
        _mix_kernel,
        grid=(n_tok // tm, D_MODEL // nc),
        in_specs=[pl.BlockSpec((tm, D_MODEL), row), pl.BlockSpec((tm, D_MODEL), row),
                  pl.BlockSpec((tm, ATTN_WIDTH), row), pl.BlockSpec((tm, CONV_WIDTH), row),
                  pl.BlockSpec((D_MODEL, nc), lambda i, j: (0, gcol + j)),
                  pl.BlockSpec((D_MODEL, nc), lambda i, j: (0, gcol + D_MODEL // nc + j)),
                  pl.BlockSpec((1, nc), lambda i, j: (0, j)),
                  pl.BlockSpec((1, nc), lambda i, j: (0, D_MODEL // nc + j)),
                  pl.BlockSpec((ATTN_WIDTH, nc), lambda i, j: (0, j)),
                  pl.BlockSpec((CONV_WIDTH, nc), lambda i, j: (0, j)),
                  pl.BlockSpec((nc, D_MODEL), lambda i, j: (j, 0)),
                  pl.BlockSpec((1, D_MODEL), lambda i, j: (0, 0)),
                  pl.BlockSpec((1, D_MODEL), lambda i, j: (0, 0))],
        out_specs=pl.BlockSpec((D_MODEL, tm), lambda i, j: (0, i)),
        out_shape=jax.ShapeDtypeStruct((D_MODEL, n_tok), F32),
        scratch_shapes=[pltpu.VMEM((tm, D_MODEL), F32)],
        compiler_params=_params("arbitrary", "arbitrary"),
        name="mix_ln1",
    )(x_bf, x_f32, attn_o, conv_o, w_in_bf, w_in_bf, b_gate, b_gate, w_ao, w_co, w_o, ln_g, ln_b)


def _merge_exchange(n):
    pairs = []
    t = max(1, math.ceil(math.log2(n)))
    p = 1 << (t - 1)
    while p > 0:
        q, r, d = 1 << (t - 1), 0, p
        while d > 0:
            pairs += [(i, i + d) for i in range(n - d) if (i & p) == r]
            d, q, r = q - p, q >> 1, p
        p >>= 1
    return pairs


def _top_values(groups, k):
    v = list(groups)
    for i, j in _merge_exchange(len(v)):
        v[i], v[j] = jnp.maximum(v[i], v[j]), jnp.minimum(v[i], v[j])
    v.append(jnp.full(v[0].shape, -jnp.inf, F32))
    out = []
    for r in range(k):
        m = jnp.max(v[0], axis=0, keepdims=True)
        out.append(m)
        pop = v[0] == m
        for d in range(min(len(v) - 1, k - 1 - r)):
            v[d] = jnp.where(pop, v[d + 1], v[d])
    return out


def _row_groups(x):
    return [x[V7X_SUBLANES * g:V7X_SUBLANES * (g + 1)] for g in range(x.shape[0] // V7X_SUBLANES)]


def _candidate_groups(a, b):
    sub = lax.broadcasted_iota(jnp.int32, (V7X_SUBLANES, a[0].shape[1]), 0)
    pad = (jnp.full_like(a[0], -jnp.inf), jnp.zeros_like(b[0]))
    rows = [(a[r1], b[r2]) for r1, r2 in _CAND] + [pad] * (N_CAND_ROWS - len(_CAND))
    sums, prods = [], []
    for g in range(N_CAND_ROWS // V7X_SUBLANES):
        grp = rows[V7X_SUBLANES * g:V7X_SUBLANES * (g + 1)]
        ca = jnp.broadcast_to(grp[0][0], sub.shape)
        cb = jnp.broadcast_to(grp[0][1], sub.shape)
        for j in range(1, V7X_SUBLANES):
            if grp[j][0] is not grp[j - 1][0]:
                ca = jnp.where(sub >= j, grp[j][0], ca)
            if grp[j][1] is not grp[j - 1][1]:
                cb = jnp.where(sub >= j, grp[j][1], cb)
        sums.append(ca + cb)
        prods.append(jnp.exp(ca - a[0]) * jnp.exp(cb - b[0]))
    return sums, prods


def _count_above(rows, x):
    n = jnp.zeros(x.shape, F32)
    for r, row in enumerate(rows):
        n = jnp.where(row > x, float(r + 1), n)
    return n


def _route_kernel(hT_ref, wqT_ref, sk_ref, n1_ref, e1_ref, r2_ref, e2_ref, qT_ref):
    qT_ref[...] = _dot(wqT_ref[...], hT_ref[...].astype(BF16))

    def head(h, carry):
        base = pl.multiple_of(h * PEER_QDIM, PEER_QDIM)
        q1 = qT_ref[pl.ds(base, PEER_HALF), :].astype(BF16)
        q2 = qT_ref[pl.ds(base + PEER_HALF, PEER_HALF), :].astype(BF16)
        s1 = _dot(sk_ref[h, 0], q1)
        s2 = _dot(sk_ref[h, 1], q2)
        a = _top_values(_row_groups(s1), N_RANKS)
        b = _top_values(_row_groups(s2), N_RANKS)
        csum, cprod = _candidate_groups(a, b)
        c = _top_values(csum, N_RANKS)
        thr = 0.5 * (c[PEER_TOPK - 1] + c[PEER_TOPK])
        picked = [jnp.where(s >= c[PEER_TOPK - 1], p, 0.0) for s, p in zip(csum, cprod)]
        denom = jnp.sum(functools.reduce(jnp.add, picked), axis=0, keepdims=True)
        split = (PEER_N_KEYS // V7X_SUBLANES, V7X_SUBLANES, s1.shape[1])
        n1_ref[h] = _count_above(b[:PEER_TOPK], thr - s1).reshape(split)
        e1_ref[h] = (jnp.exp(s1 - a[0]) * (1.0 / denom)).reshape(split)
        r2_ref[h] = _count_above(b[:PEER_TOPK], s2).astype(BF16)
        e2_ref[h] = jnp.exp(s2 - b[0]).astype(BF16)
        return carry

    lax.fori_loop(0, PEER_HEADS, head, 0)


def _peer_route(hT, wqT_bf, sub_keys_bf, tm):
    n_tok = hT.shape[1]
    groups = PEER_N_KEYS // V7X_SUBLANES
    out4 = jax.ShapeDtypeStruct((PEER_HEADS, groups, V7X_SUBLANES, n_tok), F32)
    out3 = jax.ShapeDtypeStruct((PEER_HEADS, PEER_N_KEYS, n_tok), BF16)
    spec4 = pl.BlockSpec((PEER_HEADS, groups, V7X_SUBLANES, tm), lambda i: (0, 0, 0, i))
    spec3 = pl.BlockSpec((PEER_HEADS, PEER_N_KEYS, tm), lambda i: (0, 0, i))
    return pl.pallas_call(
        _route_kernel,
        grid=(n_tok // tm,),
        in_specs=[pl.BlockSpec((D_MODEL, tm), lambda i: (0, i)),
                  pl.BlockSpec((PEER_HEADS * PEER_QDIM, D_MODEL), lambda i: (0, 0)),
                  pl.BlockSpec((PEER_HEADS, 2, PEER_N_KEYS, PEER_HALF), lambda i: (0, 0, 0, 0))],
        out_specs=[spec4, spec4, spec3, spec3],
        out_shape=[out4, out4, out3, out3],
        scratch_shapes=[pltpu.VMEM((PEER_HEADS * PEER_QDIM, tm), F32)],
        compiler_params=_params("arbitrary"),
        name="peer_route",
    )(hT, wqT_bf, sub_keys_bf)


def _gelu(x):
    return 0.5 * x * (1.0 + lax.erf(x * (2.0 ** -0.5)))


def _rows_bf16(row, n_rows):
    packed_rows = 2 * V7X_SUBLANES
    tile = jnp.broadcast_to(row, (packed_rows, row.shape[1])).astype(BF16)
    return jnp.concatenate([tile] * (n_rows // packed_rows), axis=0)


def _peer_kernel(hT_ref, u_ref, vT_ref, n1_ref, e1_ref, r2_ref, e2_ref, g_ref, b_ref, o_ref,
                 hb_ref, acc_ref):
    j = pl.program_id(1)

    @pl.when(j == 0)
    def _():
        hb_ref[...] = hT_ref[...].astype(BF16)

    act = _gelu(_dot(u_ref[...], hb_ref[...])).astype(BF16)
    slabs = []
    for i1 in range(act.shape[0] // PEER_N_KEYS):
        grp, sub = divmod(i1, V7X_SUBLANES)
        w = None
        for h in range(PEER_HEADS):
            sel = r2_ref[h] < _rows_bf16(n1_ref[h, grp, sub:sub + 1, :], PEER_N_KEYS)
            wh = jnp.where(sel, _rows_bf16(e1_ref[h, grp, sub:sub + 1, :], PEER_N_KEYS) * e2_ref[h], 0.0)
            w = wh if w is None else w + wh
        slabs.append(act[i1 * PEER_N_KEYS:(i1 + 1) * PEER_N_KEYS] * w)
    part = _dot(vT_ref[...], jnp.concatenate(slabs, axis=0))

    @pl.when(j == 0)
    def _():
        acc_ref[...] = part

    @pl.when(j > 0)
    def _():
        acc_ref[...] += part

    @pl.when(j == pl.num_programs(1) - 1)
    def _():
        r = (ALPHA * hT_ref[...] + acc_ref[...]).T
        o_ref[...] = _layer_norm(r, g_ref[...], b_ref[...])


def _peer_experts(hT, u_bf, vT_bf, n1, e1, r2, e2, ln_g, ln_b, tm, te):
    n_tok = hT.shape[1]
    groups = te // (PEER_N_KEYS * V7X_SUBLANES)
    assert groups * PEER_N_KEYS * V7X_SUBLANES == te
    tok = lambda i, j: (0, 0, i)
    blk = lambda i, j: (0, j, 0, i)
    return pl.pallas_call(
        _peer_kernel,
        grid=(n_tok // tm, PEER_N_EXPERTS // te),
        in_specs=[pl.BlockSpec((D_MODEL, tm), lambda i, j: (0, i)),
                  pl.BlockSpec((te, D_MODEL), lambda i, j: (j, 0)),
                  pl.BlockSpec((D_MODEL, te), lambda i, j: (0, j)),
                  pl.BlockSpec((PEER_HEADS, groups, V7X_SUBLANES, tm), blk),
                  pl.BlockSpec((PEER_HEADS, groups, V7X_SUBLANES, tm), blk),
                  pl.BlockSpec((PEER_HEADS, PEER_N_KEYS, tm), tok),
                  pl.BlockSpec((PEER_HEADS, PEER_N_KEYS, tm), tok),
                  pl.BlockSpec((1, D_MODEL), lambda i, j: (0, 0)),
                  pl.BlockSpec((1, D_MODEL), lambda i, j: (0, 0))],
        out_specs=pl.BlockSpec((tm, D_MODEL), lambda i, j: (i, 0)),
        out_shape=jax.ShapeDtypeStruct((n_tok, D_MODEL), F32),
        scratch_shapes=[pltpu.VMEM((D_MODEL, tm), BF16), pltpu.VMEM((D_MODEL, tm), F32)],
        compiler_params=_params("arbitrary", "arbitrary"),
        name="peer_experts",
    )(hT, u_bf, vT_bf, n1, e1, r2, e2, ln_g, ln_b)


def _layer(h, batch, seq, w_in, b_gate, sinks, conv_w, w_ao, w_co, w_o, ln1_g, ln1_b,
           w_q, sub_keys, u_tab, v_tab, ln2_g, ln2_b):
    n_tok = h.shape[0]
    t = _tiles(n_tok, seq)
    row = lambda v: v.reshape(1, -1)
    h_bf = h.astype(BF16)
    w_in_bf = w_in.astype(BF16)
    qkv = _qkv_proj(h_bf, w_in_bf, t["proj"], t["col"])
    attn_o = _attention(qkv, sinks, batch, seq)
    conv_o = _conv_branch(h_bf, w_in_bf, conv_w, seq, t["proj"], t["col"])
    hT = _mix_ln1(h_bf, h, attn_o, conv_o, w_in_bf, row(b_gate), w_ao.astype(BF16), w_co.astype(BF16),
                  w_o.astype(BF16), row(ln1_g), row(ln1_b), t["mix"], t["mix_col"])
    n1, e1, r2, e2 = _peer_route(hT, w_q.T.astype(BF16), sub_keys.astype(BF16), t["route"])
    return _peer_experts(hT, u_tab.astype(BF16), v_tab.T.astype(BF16), n1, e1, r2, e2,
                         row(ln2_g), row(ln2_b), t["peer"], t["experts"])


def kernel(x, w_in, b_gate, sinks, conv_w, w_attn_out, w_conv_out, w_o, ln1_g, ln1_b, peer_w_q,
           peer_sub_keys, peer_u, peer_v, ln2_g, ln2_b):
    batch, seq, d_model = x.shape
    assert d_model == D_MODEL and seq % WINDOW == 0
    h = x.reshape(batch * seq, d_model)
    for l in range(w_in.shape[0]):
        h = _layer(h, batch, seq, w_in[l], b_gate[l], sinks[l], conv_w[l], w_attn_out[l],
                   w_conv_out[l], w_o[l], ln1_g[l], ln1_b[l], peer_w_q[l], peer_sub_keys[l],
                   peer_u[l], peer_v[l], ln2_g[l], ln2_b[l])
    return h.reshape(batch, seq, d_model)
```

```python
import functools
import math

import jax
import jax.numpy as jnp
from jax import lax
from jax.experimental import pallas as pl
from jax.experimental.pallas import tpu as pltpu

F32 = jnp.float32
BF16 = jnp.bfloat16

D_MODEL = 2048
N_Q_HEADS = 32
N_KV_HEADS = 4
HEAD_DIM = 64
Q_PER_KV = N_Q_HEADS // N_KV_HEADS
WINDOW = 128
ATTN_WIDTH = N_Q_HEADS * HEAD_DIM
KV_WIDTH = N_KV_HEADS * HEAD_DIM
QKV_WIDTH = ATTN_WIDTH + 2 * KV_WIDTH
CONV_WIDTH = D_MODEL
CONV_K = 3
K_OFF = ATTN_WIDTH
V_OFF = K_OFF + KV_WIDTH
CB_OFF = V_OFF + KV_WIDTH
CC_OFF = CB_OFF + CONV_WIDTH
CH_OFF = CC_OFF + CONV_WIDTH
G_OFF = CH_OFF + CONV_WIDTH

PEER_HEADS = 8
PEER_N_KEYS = 128
PEER_N_EXPERTS = PEER_N_KEYS * PEER_N_KEYS
PEER_TOPK = 16
PEER_HALF = 128
PEER_QDIM = 2 * PEER_HALF

LN_EPS = 1e-5
DEPTH = 1
ALPHA = (2.0 * DEPTH) ** 0.25

V7X_LANES = 128
V7X_SUBLANES = 8
V7X_VMEM_LIMIT_BYTES = 56 * 1024 * 1024

N_RANKS = PEER_TOPK + 1
_CAND = [(r1, r2) for r1 in range(N_RANKS) for r2 in range(N_RANKS) if (r1 + 1) * (r2 + 1) <= N_RANKS]
N_CAND_ROWS = -(-len(_CAND) // V7X_SUBLANES) * V7X_SUBLANES


def _tiles(n_tokens, seq):
    return dict(
        proj=min(1024, seq),
        mix=min(512, n_tokens),
        route=min(256, n_tokens),
        peer=min(512, n_tokens),
        col=512,
        mix_col=256,
        experts=1024,
    )


def _params(*sem):
    return pltpu.CompilerParams(dimension_semantics=sem, vmem_limit_bytes=V7X_VMEM_LIMIT_BYTES)


def _dot(a, b):
    return jnp.dot(a, b, preferred_element_type=F32)


def _matmul_kernel(x_ref, w_ref, o_ref):
    o_ref[...] = _dot(x_ref[...], w_ref[...]).astype(o_ref.dtype)


def _qkv_proj(x_bf, w_in_bf, tm, tn):
    n_tok = x_bf.shape[0]
    return pl.pallas_call(
        _matmul_kernel,
        grid=(QKV_WIDTH // tn, n_tok // tm),
        in_specs=[pl.BlockSpec((tm, D_MODEL), lambda j, i: (i, 0)),
                  pl.BlockSpec((D_MODEL, tn), lambda j, i: (0, j))],
        out_specs=pl.BlockSpec((tm, tn), lambda j, i: (i, j)),
        out_shape=jax.ShapeDtypeStruct((n_tok, QKV_WIDTH), BF16),
        compiler_params=_params("arbitrary", "arbitrary"),
        name="qkv_proj",
    )(x_bf, w_in_bf)


def _conv_kernel(x_ref, wcb_ref, wcc_ref, wch_ref, cw_ref, o_ref, carry_ref, *, tiles_per_seq):
    i = pl.program_id(1)
    x = x_ref[...]
    cb = _dot(x, wcb_ref[...])
    u = _dot(x, wcc_ref[...]) * _dot(x, wch_ref[...])
    tm = u.shape[0]

    @pl.when(i % tiles_per_seq == 0)
    def _():
        carry_ref[...] = jnp.zeros_like(carry_ref)

    prev = carry_ref[...]
    row = lax.broadcasted_iota(jnp.int32, u.shape, 0)
    u1 = jnp.where(row == 0, prev[7:8], pltpu.roll(u, 1, 0))
    u2 = pltpu.roll(u, 2, 0)
    u2 = jnp.where(row == 0, prev[6:7], jnp.where(row == 1, prev[7:8], u2))
    cw = cw_ref[...]
    y = cb * (cw[0:1] * u2 + cw[1:2] * u1 + cw[2:3] * u)
    o_ref[...] = y.astype(o_ref.dtype)
    carry_ref[...] = u[tm - V7X_SUBLANES:tm]


def _conv_branch(x_bf, w_in_bf, conv_w, seq, tm, nc):
    n_tok = x_bf.shape[0]
    wspec = lambda off: pl.BlockSpec((D_MODEL, nc), lambda j, i, o=off // nc: (0, o + j))
    return pl.pallas_call(
        functools.partial(_conv_kernel, tiles_per_seq=seq // tm),
        grid=(CONV_WIDTH // nc, n_tok // tm),
        in_specs=[pl.BlockSpec((tm, D_MODEL), lambda j, i: (i, 0)),
                  wspec(CB_OFF), wspec(CC_OFF), wspec(CH_OFF),
                  pl.BlockSpec((CONV_K, nc), lambda j, i: (0, j))],
        out_specs=pl.BlockSpec((tm, nc), lambda j, i: (i, j)),
        out_shape=jax.ShapeDtypeStruct((n_tok, CONV_WIDTH), BF16),
        scratch_shapes=[pltpu.VMEM((V7X_SUBLANES, nc), F32)],
        compiler_params=_params("arbitrary", "arbitrary"),
        name="conv_branch",
    )(x_bf, w_in_bf, w_in_bf, w_in_bf, conv_w)


def _attn_kernel(sinks_ref, q_ref, kc_ref, kp_ref, vc_ref, vp_ref, o_ref):
    n = pl.program_id(1)
    cols = Q_PER_KV * WINDOW
    kj = lax.broadcasted_iota(jnp.int32, (2 * WINDOW, cols), 0)
    qi = lax.broadcasted_iota(jnp.int32, (2 * WINDOW, cols), 1) & (WINDOW - 1)
    rel = WINDOW + qi - kj
    mask = (rel >= 0) & (rel < WINDOW) & ((kj >= WINDOW) | (n > 0))
    scale = HEAD_DIM ** -0.5
    for j in range(N_KV_HEADS):
        heads = [j * Q_PER_KV + g for g in range(Q_PER_KV)]
        qg = jnp.concatenate([q_ref[:, h * HEAD_DIM:(h + 1) * HEAD_DIM] for h in heads], axis=0)
        ksl = slice(j * HEAD_DIM, (j + 1) * HEAD_DIM)
        kw = jnp.concatenate([kp_ref[:, ksl], kc_ref[:, ksl]], axis=0)
        vw = jnp.concatenate([vp_ref[:, ksl], vc_ref[:, ksl]], axis=0)
        s = lax.dot_general(kw, qg * scale, (((1,), (1,)), ((), ())), preferred_element_type=F32)
        s = jnp.where(mask, s, -jnp.inf)
        sink = jnp.concatenate([jnp.full((1, WINDOW), sinks_ref[h], F32) for h in heads], axis=1)
        m = jnp.maximum(jnp.max(s, axis=0, keepdims=True), sink)
        p = jnp.exp(s - m)
        denom = jnp.sum(p, axis=0, keepdims=True) + jnp.exp(sink - m)
        probs = (p / denom).astype(BF16)
        o = lax.dot_general(probs, vw, (((0,), (0,)), ((), ())), preferred_element_type=F32)
        for g, h in enumerate(heads):
            o_ref[:, h * HEAD_DIM:(h + 1) * HEAD_DIM] = o[g * WINDOW:(g + 1) * WINDOW].astype(o_ref.dtype)


def _attention(qkv, sinks, batch, seq):
    n_tok = qkv.shape[0]
    nb = seq // WINDOW
    cur = lambda b, n: b * nb + n
    prev = lambda b, n: b * nb + jnp.maximum(n - 1, 0)
    kcol, vcol = K_OFF // KV_WIDTH, V_OFF // KV_WIDTH
    return pl.pallas_call(
        _attn_kernel,
        grid=(batch, nb),
        in_specs=[pl.BlockSpec(memory_space=pltpu.SMEM),
                  pl.BlockSpec((WINDOW, ATTN_WIDTH), lambda b, n: (cur(b, n), 0)),
                  pl.BlockSpec((WINDOW, KV_WIDTH), lambda b, n: (cur(b, n), kcol)),
                  pl.BlockSpec((WINDOW, KV_WIDTH), lambda b, n: (prev(b, n), kcol)),
                  pl.BlockSpec((WINDOW, KV_WIDTH), lambda b, n: (cur(b, n), vcol)),
                  pl.BlockSpec((WINDOW, KV_WIDTH), lambda b, n: (prev(b, n), vcol))],
        out_specs=pl.BlockSpec((WINDOW, ATTN_WIDTH), lambda b, n: (cur(b, n), 0)),
        out_shape=jax.ShapeDtypeStruct((n_tok, ATTN_WIDTH), BF16),
        compiler_params=_params("arbitrary", "arbitrary"),
        name="swa_attention",
    )(sinks, qkv, qkv, qkv, qkv, qkv)


def _layer_norm(r, g, b):
    mu = jnp.mean(r, axis=-1, keepdims=True)
    d = r - mu
    var = jnp.mean(d * d, axis=-1, keepdims=True)
    return d * lax.rsqrt(var + LN_EPS) * g + b


def _sigmoid(z):
    return 1.0 / (1.0 + jnp.exp(-z))


def _mix_kernel(xb_ref, xf_ref, a_ref, c_ref, wga_ref, wgc_ref, bga_ref, bgc_ref, wao_ref, wco_ref,
                wo_ref, g_ref, b_ref, hT_ref, acc_ref):
    j = pl.program_id(1)

    @pl.when(j == 0)
    def _():
        acc_ref[...] = jnp.zeros_like(acc_ref)

    xb = xb_ref[...]
    ga = _sigmoid(_dot(xb, wga_ref[...]) + bga_ref[...])
    gc = _sigmoid(_dot(xb, wgc_ref[...]) + bgc_ref[...])
    merged = ga * _dot(a_ref[...], wao_ref[...]) + gc * _dot(c_ref[...], wco_ref[...])
    acc_ref[...] += _dot(merged.astype(BF16), wo_ref[...])

    @pl.when(j == pl.num_programs(1) - 1)
    def _():
        h1 = _layer_norm(ALPHA * xf_ref[...] + acc_ref[...], g_ref[...], b_ref[...])
        hT_ref[...] = h1.T


def _mix_ln1(x_bf, x_f32, attn_o, conv_o, w_in_bf, b_gate, w_ao, w_co, w_o, ln_g, ln_b, tm, nc):
    n_tok = x_bf.shape[0]
    row = lambda i, j: (i, 0)
    gcol = G_OFF // nc
    return pl.pallas_call(
        _mix_kernel,
        grid=(n_tok // tm, D_MODEL // nc),
        in_specs=[pl.BlockSpec((tm, D_MODEL), row), pl.BlockSpec((tm, D_MODEL), row),
                  pl.BlockSpec((tm, ATTN_WIDTH), row), pl.BlockSpec((tm, CONV_WIDTH), row),
                  pl.BlockSpec((D_MODEL, nc), lambda i, j: (0, gcol + j)),
                  pl.BlockSpec((D_MODEL, nc), lambda i, j: (0, gcol + D_MODEL // nc + j)),
                  pl.BlockSpec((1, nc), lambda i, j: (0, j)),
                  pl.BlockSpec((1, nc), lambda i, j: (0, D_MODEL // nc + j)),
                  pl.BlockSpec((ATTN_WIDTH, nc), lambda i, j: (0, j)),
                  pl.BlockSpec((CONV_WIDTH, nc), lambda i, j: (0, j)),
                  pl.BlockSpec((nc, D_MODEL), lambda i, j: (j, 0)),
                  pl.BlockSpec((1, D_MODEL), lambda i, j: (0, 0)),
                  pl.BlockSpec((1, D_MODEL), lambda i, j: (0, 0))],
        out_specs=pl.BlockSpec((D_MODEL, tm), lambda i, j: (0, i)),
        out_shape=jax.ShapeDtypeStruct((D_MODEL, n_tok), F32),
        scratch_shapes=[pltpu.VMEM((tm, D_MODEL), F32)],
        compiler_params=_params("arbitrary", "arbitrary"),
        name="mix_ln1",
    )(x_bf, x_f32, attn_o, conv_o, w_in_bf, w_in_bf, b_gate, b_gate, w_ao, w_co, w_o, ln_g, ln_b)


def _merge_exchange(n):
    pairs = []
    t = max(1, math.ceil(math.log2(n)))
    p = 1 << (t - 1)
    while p > 0:
        q, r, d = 1 << (t - 1), 0, p
        while d > 0:
            pairs += [(i, i + d) for i in range(n - d) if (i & p) == r]
            d, q, r = q - p, q >> 1, p
        p >>= 1
    return pairs


def _top_values(groups, k):
    v = list(groups)
    for i, j in _merge_exchange(len(v)):
        v[i], v[j] = jnp.maximum(v[i], v[j]), jnp.minimum(v[i], v[j])
    v.append(jnp.full(v[0].shape, -jnp.inf, F32))
    out = []
    for r in range(k):
        m = jnp.max(v[0], axis=0, keepdims=True)
        out.append(m)
        pop = v[0] == m
        for d in range(min(len(v) - 1, k - 1 - r)):
            v[d] = jnp.where(pop, v[d + 1], v[d])
    return out


def _row_groups(x):
    return [x[V7X_SUBLANES * g:V7X_SUBLANES * (g + 1)] for g in range(x.shape[0] // V7X_SUBLANES)]


def _candidate_groups(a, b):
    sub = lax.broadcasted_iota(jnp.int32, (V7X_SUBLANES, a[0].shape[1]), 0)
    pad = (jnp.full_like(a[0], -jnp.inf), jnp.zeros_like(b[0]))
    rows = [(a[r1], b[r2]) for r1, r2 in _CAND] + [pad] * (N_CAND_ROWS - len(_CAND))
    sums, prods = [], []
    for g in range(N_CAND_ROWS // V7X_SUBLANES):
        grp = rows[V7X_SUBLANES * g:V7X_SUBLANES * (g + 1)]
        ca = jnp.broadcast_to(grp[0][0], sub.shape)
        cb = jnp.broadcast_to(grp[0][1], sub.shape)
        for j in range(1, V7X_SUBLANES):
            if grp[j][0] is not grp[j - 1][0]:
                ca = jnp.where(sub >= j, grp[j][0], ca)
            if grp[j][1] is not grp[j - 1][1]:
                cb = jnp.where(sub >= j, grp[j][1], cb)
        sums.append(ca + cb)
        prods.append(jnp.exp(ca - a[0]) * jnp.exp(cb - b[0]))
    return sums, prods


def _count_above(rows, x):
    n = jnp.zeros(x.shape, F32)
    for r, row in enumerate(rows):
        n = jnp.where(row > x, float(r + 1), n)
    return n


def _route_kernel(hT_ref, wqT_ref, sk_ref, n1_ref, e1_ref, r2_ref, e2_ref, qT_ref):
    qT_ref[...] = _dot(wqT_ref[...], hT_ref[...].astype(BF16))

    def head(h, carry):
        base = pl.multiple_of(h * PEER_QDIM, PEER_QDIM)
        q1 = qT_ref[pl.ds(base, PEER_HALF), :].astype(BF16)
        q2 = qT_ref[pl.ds(base + PEER_HALF, PEER_HALF), :].astype(BF16)
        s1 = _dot(sk_ref[h, 0], q1)
        s2 = _dot(sk_ref[h, 1], q2)
        a = _top_values(_row_groups(s1), N_RANKS)
        b = _top_values(_row_groups(s2), N_RANKS)
        csum, cprod = _candidate_groups(a, b)
        c = _top_values(csum, N_RANKS)
        thr = 0.5 * (c[PEER_TOPK - 1] + c[PEER_TOPK])
        picked = [jnp.where(s >= c[PEER_TOPK - 1], p, 0.0) for s, p in zip(csum, cprod)]
        denom = jnp.sum(functools.reduce(jnp.add, picked), axis=0, keepdims=True)
        split = (PEER_N_KEYS // V7X_SUBLANES, V7X_SUBLANES, s1.shape[1])
        n1_ref[h] = _count_above(b[:PEER_TOPK], thr - s1).reshape(split)
        e1_ref[h] = (jnp.exp(s1 - a[0]) * (1.0 / denom)).reshape(split)
        r2_ref[h] = _count_above(b[:PEER_TOPK], s2).astype(BF16)
        e2_ref[h] = jnp.exp(s2 - b[0]).astype(BF16)
        return carry

    lax.fori_loop(0, PEER_HEADS, head, 0)


def _peer_route(hT, wqT_bf, sub_keys_bf, tm):
    n_tok = hT.shape[1]
    groups = PEER_N_KEYS // V7X_SUBLANES
    out4 = jax.ShapeDtypeStruct((PEER_HEADS, groups, V7X_SUBLANES, n_tok), F32)
    out3 = jax.ShapeDtypeStruct((PEER_HEADS, PEER_N_KEYS, n_tok), BF16)
    spec4 = pl.BlockSpec((PEER_HEADS, groups, V7X_SUBLANES, tm), lambda i: (0, 0, 0, i))
    spec3 = pl.BlockSpec((PEER_HEADS, PEER_N_KEYS, tm), lambda i: (0, 0, i))
    return pl.pallas_call(
        _route_kernel,
        grid=(n_tok // tm,),
        in_specs=[pl.BlockSpec((D_MODEL, tm), lambda i: (0, i)),
                  pl.BlockSpec((PEER_HEADS * PEER_QDIM, D_MODEL), lambda i: (0, 0)),
                  pl.BlockSpec((PEER_HEADS, 2, PEER_N_KEYS, PEER_HALF), lambda i: (0, 0, 0, 0))],
        out_specs=[spec4, spec4, spec3, spec3],
        out_shape=[out4, out4, out3, out3],
        scratch_shapes=[pltpu.VMEM((PEER_HEADS * PEER_QDIM, tm), F32)],
        compiler_params=_params("arbitrary"),
        name="peer_route",
    )(hT, wqT_bf, sub_keys_bf)


def _gelu(x):
    return 0.5 * x * (1.0 + lax.erf(x * (2.0 ** -0.5)))


def _rows_bf16(row, n_rows):
    packed_rows = 2 * V7X_SUBLANES
    tile = jnp.broadcast_to(row, (packed_rows, row.shape[1])).astype(BF16)
    return jnp.concatenate([tile] * (n_rows // packed_rows), axis=0)


PEER_OUT_CHUNKS = 1


def _peer_kernel(hT_ref, u_ref, vT_ref, n1_ref, e1_ref, r2_ref, e2_ref, g_ref, b_ref, o_ref,
                 hb_ref, acc_ref):
    j = pl.program_id(1)

    @pl.when(j == 0)
    def _():
        hb_ref[...] = hT_ref[...].astype(BF16)
        acc_ref[...] = jnp.zeros_like(acc_ref)

    act = _gelu(_dot(u_ref[...], hb_ref[...])).astype(BF16)
    slabs = []
    for i1 in range(act.shape[0] // PEER_N_KEYS):
        grp, sub = divmod(i1, V7X_SUBLANES)
        w = None
        for h in range(PEER_HEADS):
            sel = r2_ref[h] < _rows_bf16(n1_ref[h, grp, sub:sub + 1, :], PEER_N_KEYS)
            wh = jnp.where(sel, _rows_bf16(e1_ref[h, grp, sub:sub + 1, :], PEER_N_KEYS) * e2_ref[h], 0.0)
            w = wh if w is None else w + wh
        slabs.append(act[i1 * PEER_N_KEYS:(i1 + 1) * PEER_N_KEYS] * w)
    a = jnp.concatenate(slabs, axis=0)
    rows = acc_ref.shape[0] // PEER_OUT_CHUNKS
    for c in range(PEER_OUT_CHUNKS):
        out = slice(c * rows, (c + 1) * rows)
        acc_ref[out, :] += _dot(vT_ref[out, :], a)

    @pl.when(j == pl.num_programs(1) - 1)
    def _():
        r = (ALPHA * hT_ref[...] + acc_ref[...]).T
        o_ref[...] = _layer_norm(r, g_ref[...], b_ref[...])


def _peer_experts(hT, u_bf, vT_bf, n1, e1, r2, e2, ln_g, ln_b, tm, te):
    n_tok = hT.shape[1]
    groups = te // (PEER_N_KEYS * V7X_SUBLANES)
    assert groups * PEER_N_KEYS * V7X_SUBLANES == te
    tok = lambda i, j: (0, 0, i)
    blk = lambda i, j: (0, j, 0, i)
    return pl.pallas_call(
        _peer_kernel,
        grid=(n_tok // tm, PEER_N_EXPERTS // te),
        in_specs=[pl.BlockSpec((D_MODEL, tm), lambda i, j: (0, i)),
                  pl.BlockSpec((te, D_MODEL), lambda i, j: (j, 0)),
                  pl.BlockSpec((D_MODEL, te), lambda i, j: (0, j)),
                  pl.BlockSpec((PEER_HEADS, groups, V7X_SUBLANES, tm), blk),
                  pl.BlockSpec((PEER_HEADS, groups, V7X_SUBLANES, tm), blk),
                  pl.BlockSpec((PEER_HEADS, PEER_N_KEYS, tm), tok),
                  pl.BlockSpec((PEER_HEADS, PEER_N_KEYS, tm), tok),
                  pl.BlockSpec((1, D_MODEL), lambda i, j: (0, 0)),
                  pl.BlockSpec((1, D_MODEL), lambda i, j: (0, 0))],
        out_specs=pl.BlockSpec((tm, D_MODEL), lambda i, j: (i, 0)),
        out_shape=jax.ShapeDtypeStruct((n_tok, D_MODEL), F32),
        scratch_shapes=[pltpu.VMEM((D_MODEL, tm), BF16), pltpu.VMEM((D_MODEL, tm), F32)],
        compiler_params=_params("arbitrary", "arbitrary"),
        name="peer_experts",
    )(hT, u_bf, vT_bf, n1, e1, r2, e2, ln_g, ln_b)


def _layer(h, batch, seq, w_in, b_gate, sinks, conv_w, w_ao, w_co, w_o, ln1_g, ln1_b,
           w_q, sub_keys, u_tab, v_tab, ln2_g, ln2_b):
    n_tok = h.shape[0]
    t = _tiles(n_tok, seq)
    row = lambda v: v.reshape(1, -1)
    h_bf = h.astype(BF16)
    w_in_bf = w_in.astype(BF16)
    qkv = _qkv_proj(h_bf, w_in_bf, t["proj"], t["col"])
    attn_o = _attention(qkv, sinks, batch, seq)
    conv_o = _conv_branch(h_bf, w_in_bf, conv_w, seq, t["proj"], t["col"])
    hT = _mix_ln1(h_bf, h, attn_o, conv_o, w_in_bf, row(b_gate), w_ao.astype(BF16), w_co.astype(BF16),
                  w_o.astype(BF16), row(ln1_g), row(ln1_b), t["mix"], t["mix_col"])
    n1, e1, r2, e2 = _peer_route(hT, w_q.T.astype(BF16), sub_keys.astype(BF16), t["route"])
    return _peer_experts(hT, u_tab.astype(BF16), v_tab.T.astype(BF16), n1, e1, r2, e2,
                         row(ln2_g), row(ln2_b), t["peer"], t["experts"])


def kernel(x, w_in, b_gate, sinks, conv_w, w_attn_out, w_conv_out, w_o, ln1_g, ln1_b, peer_w_q,
           peer_sub_keys, peer_u, peer_v, ln2_g, ln2_b):
    batch, seq, d_model = x.shape
    assert d_model == D_MODEL and seq % WINDOW == 0
    h = x.reshape(batch * seq, d_model)
    for l in range(w_in.shape[0]):
        h = _layer(h, batch, seq, w_in[l], b_gate[l], sinks[l], conv_w[l], w_attn_out[l],
                   w_conv_out[l], w_o[l], ln1_g[l], ln1_b[l], peer_w_q[l], peer_sub_keys[l],
                   peer_u[l], peer_v[l], ln2_g[l], ln2_b[l])
    return h.reshape(batch, seq, d_model)
```

```python
import functools
import math

import jax
import jax.numpy as jnp
from jax import lax
from jax.experimental import pallas as pl
from jax.experimental.pallas import tpu as pltpu

F32 = jnp.float32
BF16 = jnp.bfloat16

D_MODEL = 2048
N_Q_HEADS = 32
N_KV_HEADS = 4
HEAD_DIM = 64
Q_PER_KV = N_Q_HEADS // N_KV_HEADS
WINDOW = 128
ATTN_WIDTH = N_Q_HEADS * HEAD_DIM
KV_WIDTH = N_KV_HEADS * HEAD_DIM
QKV_WIDTH = ATTN_WIDTH + 2 * KV_WIDTH
CONV_WIDTH = D_MODEL
CONV_K = 3
K_OFF = ATTN_WIDTH
V_OFF = K_OFF + KV_WIDTH
CB_OFF = V_OFF + KV_WIDTH
CC_OFF = CB_OFF + CONV_WIDTH
CH_OFF = CC_OFF + CONV_WIDTH
G_OFF = CH_OFF + CONV_WIDTH

PEER_HEADS = 8
PEER_N_KEYS = 128
PEER_N_EXPERTS = PEER_N_KEYS * PEER_N_KEYS
PEER_TOPK = 16
PEER_HALF = 128
PEER_QDIM = 2 * PEER_HALF

LN_EPS = 1e-5
DEPTH = 1
ALPHA = (2.0 * DEPTH) ** 0.25

V7X_LANES = 128
V7X_SUBLANES = 8
V7X_VMEM_LIMIT_BYTES = 56 * 1024 * 1024

N_RANKS = PEER_TOPK + 1
_CAND = [(r1, r2) for r1 in range(N_RANKS) for r2 in range(N_RANKS) if (r1 + 1) * (r2 + 1) <= N_RANKS]
N_CAND_ROWS = -(-len(_CAND) // V7X_SUBLANES) * V7X_SUBLANES


def _tiles(n_tokens, seq):
    return dict(
        proj=min(1024, seq),
        mix=min(512, n_tokens),
        route=min(256, n_tokens),
        peer=min(512, n_tokens),
        col=512,
        mix_col=256,
        experts=1024,
    )


def _params(*sem):
    return pltpu.CompilerParams(dimension_semantics=sem, vmem_limit_bytes=V7X_VMEM_LIMIT_BYTES)


def _dot(a, b):
    return jnp.dot(a, b, preferred_element_type=F32)


def _qkv_kernel(x_ref, w_ref, o_ref, wb_ref):
    @pl.when(pl.program_id(1) == 0)
    def _():
        wb_ref[...] = w_ref[...].astype(BF16)

    o_ref[...] = _dot(x_ref[...], wb_ref[...]).astype(o_ref.dtype)


def _qkv_proj(x_bf, w_in, tm, tn):
    n_tok = x_bf.shape[0]
    return pl.pallas_call(
        _qkv_kernel,
        grid=(QKV_WIDTH // tn, n_tok // tm),
        in_specs=[pl.BlockSpec((tm, D_MODEL), lambda j, i: (i, 0)),
                  pl.BlockSpec((D_MODEL, tn), lambda j, i: (0, j))],
        out_specs=pl.BlockSpec((tm, tn), lambda j, i: (i, j)),
        out_shape=jax.ShapeDtypeStruct((n_tok, QKV_WIDTH), BF16),
        scratch_shapes=[pltpu.VMEM((D_MODEL, tn), BF16)],
        compiler_params=_params("arbitrary", "arbitrary"),
        name="qkv_proj",
    )(x_bf, w_in)


def _conv_kernel(x_ref, wcb_ref, wcc_ref, wch_ref, cw_ref, o_ref, carry_ref, *, tiles_per_seq):
    i = pl.program_id(1)
    x = x_ref[...]
    cb = _dot(x, wcb_ref[...])
    u = _dot(x, wcc_ref[...]) * _dot(x, wch_ref[...])
    tm = u.shape[0]

    @pl.when(i % tiles_per_seq == 0)
    def _():
        carry_ref[...] = jnp.zeros_like(carry_ref)

    prev = carry_ref[...]
    row = lax.broadcasted_iota(jnp.int32, u.shape, 0)
    u1 = jnp.where(row == 0, prev[7:8], pltpu.roll(u, 1, 0))
    u2 = pltpu.roll(u, 2, 0)
    u2 = jnp.where(row == 0, prev[6:7], jnp.where(row == 1, prev[7:8], u2))
    cw = cw_ref[...]
    y = cb * (cw[0:1] * u2 + cw[1:2] * u1 + cw[2:3] * u)
    o_ref[...] = y.astype(o_ref.dtype)
    carry_ref[...] = u[tm - V7X_SUBLANES:tm]


def _conv_branch(x_bf, w_in_bf, conv_w, seq, tm, nc):
    n_tok = x_bf.shape[0]
    wspec = lambda off: pl.BlockSpec((D_MODEL, nc), lambda j, i, o=off // nc: (0, o + j))
    return pl.pallas_call(
        functools.partial(_conv_kernel, tiles_per_seq=seq // tm),
        grid=(CONV_WIDTH // nc, n_tok // tm),
        in_specs=[pl.BlockSpec((tm, D_MODEL), lambda j, i: (i, 0)),
                  wspec(CB_OFF), wspec(CC_OFF), wspec(CH_OFF),
                  pl.BlockSpec((CONV_K, nc), lambda j, i: (0, j))],
        out_specs=pl.BlockSpec((tm, nc), lambda j, i: (i, j)),
        out_shape=jax.ShapeDtypeStruct((n_tok, CONV_WIDTH), BF16),
        scratch_shapes=[pltpu.VMEM((V7X_SUBLANES, nc), F32)],
        compiler_params=_params("arbitrary", "arbitrary"),
        name="conv_branch",
    )(x_bf, w_in_bf, w_in_bf, w_in_bf, conv_w)


def _attn_kernel(sinks_ref, q_ref, kc_ref, kp_ref, vc_ref, vp_ref, *refs):
    n_side = (len(refs) - 1) // 2
    o_ref = refs[n_side]
    for src, dst in zip(refs[:n_side], refs[n_side + 1:]):
        dst[...] = src[...].astype(BF16)
    n = pl.program_id(1)
    cols = Q_PER_KV * WINDOW
    kj = lax.broadcasted_iota(jnp.int32, (2 * WINDOW, cols), 0)
    qi = lax.broadcasted_iota(jnp.int32, (2 * WINDOW, cols), 1) & (WINDOW - 1)
    rel = WINDOW + qi - kj
    mask = (rel >= 0) & (rel < WINDOW) & ((kj >= WINDOW) | (n > 0))
    scale = HEAD_DIM ** -0.5
    for j in range(N_KV_HEADS):
        heads = [j * Q_PER_KV + g for g in range(Q_PER_KV)]
        qg = jnp.concatenate([q_ref[:, h * HEAD_DIM:(h + 1) * HEAD_DIM] for h in heads], axis=0)
        ksl = slice(j * HEAD_DIM, (j + 1) * HEAD_DIM)
        kw = jnp.concatenate([kp_ref[:, ksl], kc_ref[:, ksl]], axis=0)
        vw = jnp.concatenate([vp_ref[:, ksl], vc_ref[:, ksl]], axis=0)
        s = lax.dot_general(kw, qg * scale, (((1,), (1,)), ((), ())), preferred_element_type=F32)
        s = jnp.where(mask, s, -jnp.inf)
        sink = jnp.concatenate([jnp.full((1, WINDOW), sinks_ref[h], F32) for h in heads], axis=1)
        m = jnp.maximum(jnp.max(s, axis=0, keepdims=True), sink)
        p = jnp.exp(s - m)
        denom = jnp.sum(p, axis=0, keepdims=True) + jnp.exp(sink - m)
        probs = (p / denom).astype(BF16)
        o = lax.dot_general(probs, vw, (((0,), (0,)), ((), ())), preferred_element_type=F32)
        for g, h in enumerate(heads):
            o_ref[:, h * HEAD_DIM:(h + 1) * HEAD_DIM] = o[g * WINDOW:(g + 1) * WINDOW].astype(o_ref.dtype)


ATTN_SIDE_ROWS_MAX = 64


def _attention(qkv, sinks, batch, seq, weights):
    n_tok = qkv.shape[0]
    nb = seq // WINDOW
    cur = lambda b, n: b * nb + n
    prev = lambda b, n: b * nb + jnp.maximum(n - 1, 0)
    kcol, vcol = K_OFF // KV_WIDTH, V_OFF // KV_WIDTH
    rows = D_MODEL // (batch * nb)
    fuse = (rows * batch * nb == D_MODEL and rows % (2 * V7X_SUBLANES) == 0 and rows <= ATTN_SIDE_ROWS_MAX)
    side = weights if fuse else []
    wspec = lambda w: pl.BlockSpec((rows, w.shape[1]), lambda b, n: (cur(b, n), 0))
    outs = pl.pallas_call(
        _attn_kernel,
        grid=(batch, nb),
        in_specs=[pl.BlockSpec(memory_space=pltpu.SMEM),
                  pl.BlockSpec((WINDOW, ATTN_WIDTH), lambda b, n: (cur(b, n), 0)),
                  pl.BlockSpec((WINDOW, KV_WIDTH), lambda b, n: (cur(b, n), kcol)),
                  pl.BlockSpec((WINDOW, KV_WIDTH), lambda b, n: (prev(b, n), kcol)),
                  pl.BlockSpec((WINDOW, KV_WIDTH), lambda b, n: (cur(b, n), vcol)),
                  pl.BlockSpec((WINDOW, KV_WIDTH), lambda b, n: (prev(b, n), vcol))]
        + [wspec(w) for w in side],
        out_specs=[pl.BlockSpec((WINDOW, ATTN_WIDTH), lambda b, n: (cur(b, n), 0))] + [wspec(w) for w in side],
        out_shape=[jax.ShapeDtypeStruct((n_tok, ATTN_WIDTH), BF16)]
        + [jax.ShapeDtypeStruct(w.shape, BF16) for w in side],
        compiler_params=_params("arbitrary", "arbitrary"),
        name="swa_attention",
    )(sinks, qkv, qkv, qkv, qkv, qkv, *side)
    if not fuse:
        outs = list(outs) + [w.astype(BF16) for w in weights]
    return outs


def _layer_norm(r, g, b):
    mu = jnp.mean(r, axis=-1, keepdims=True)
    d = r - mu
    var = jnp.mean(d * d, axis=-1, keepdims=True)
    return d * lax.rsqrt(var + LN_EPS) * g + b


def _sigmoid(z):
    return 1.0 / (1.0 + jnp.exp(-z))


def _mix_kernel(xb_ref, xf_ref, a_ref, c_ref, wga_ref, wgc_ref, bga_ref, bgc_ref, wao_ref, wco_ref,
                wo_ref, g_ref, b_ref, hT_ref, acc_ref):
    j = pl.program_id(1)

    @pl.when(j == 0)
    def _():
        acc_ref[...] = jnp.zeros_like(acc_ref)

    xb = xb_ref[...]
    ga = _sigmoid(_dot(xb, wga_ref[...]) + bga_ref[...])
    gc = _sigmoid(_dot(xb, wgc_ref[...]) + bgc_ref[...])
    merged = ga * _dot(a_ref[...], wao_ref[...]) + gc * _dot(c_ref[...], wco_ref[...])
    acc_ref[...] += _dot(merged.astype(BF16), wo_ref[...])

    @pl.when(j == pl.num_programs(1) - 1)
    def _():
        h1 = _layer_norm(ALPHA * xf_ref[...] + acc_ref[...], g_ref[...], b_ref[...])
        hT_ref[...] = h1.T


def _mix_ln1(x_bf, x_f32, attn_o, conv_o, w_in_bf, b_gate, w_ao, w_co, w_o, ln_g, ln_b, tm, nc):
    n_tok = x_bf.shape[0]
    row = lambda i, j: (i, 0)
    gcol = G_OFF // nc
    return pl.pallas_call(
        _mix_kernel,
        grid=(n_tok // tm, D_MODEL // nc),
        in_specs=[pl.BlockSpec((tm, D_MODEL), row), pl.BlockSpec((tm, D_MODEL), row),
                  pl.BlockSpec((tm, ATTN_WIDTH), row), pl.BlockSpec((tm, CONV_WIDTH), row),
                  pl.BlockSpec((D_MODEL, nc), lambda i, j: (0, gcol + j)),
                  pl.BlockSpec((D_MODEL, nc), lambda i, j: (0, gcol + D_MODEL // nc + j)),
                  pl.BlockSpec((1, nc), lambda i, j: (0, j)),
                  pl.BlockSpec((1, nc), lambda i, j: (0, D_MODEL // nc + j)),
                  pl.BlockSpec((ATTN_WIDTH, nc), lambda i, j: (0, j)),
                  pl.BlockSpec((CONV_WIDTH, nc), lambda i, j: (0, j)),
                  pl.BlockSpec((nc, D_MODEL), lambda i, j: (j, 0)),
                  pl.BlockSpec((1, D_MODEL), lambda i, j: (0, 0)),
                  pl.BlockSpec((1, D_MODEL), lambda i, j: (0, 0))],
        out_specs=pl.BlockSpec((D_MODEL, tm), lambda i, j: (0, i)),
        out_shape=jax.ShapeDtypeStruct((D_MODEL, n_tok), F32),
        scratch_shapes=[pltpu.VMEM((tm, D_MODEL), F32)],
        compiler_params=_params("arbitrary", "arbitrary"),
        name="mix_ln1",
    )(x_bf, x_f32, attn_o, conv_o, w_in_bf, w_in_bf, b_gate, b_gate, w_ao, w_co, w_o, ln_g, ln_b)


def _merge_exchange(n):
    pairs = []
    t = max(1, math.ceil(math.log2(n)))
    p = 1 << (t - 1)
    while p > 0:
        q, r, d = 1 << (t - 1), 0, p
        while d > 0:
            pairs += [(i, i + d) for i in range(n - d) if (i & p) == r]
            d, q, r = q - p, q >> 1, p
        p >>= 1
    return pairs


def _top_values(groups, k):
    v = list(groups)
    for i, j in _merge_exchange(len(v)):
        v[i], v[j] = jnp.maximum(v[i], v[j]), jnp.minimum(v[i], v[j])
    v.append(jnp.full(v[0].shape, -jnp.inf, F32))
    out = []
    for r in range(k):
        m = jnp.max(v[0], axis=0, keepdims=True)
        out.append(m)
        pop = v[0] == m
        for d in range(min(len(v) - 1, k - 1 - r)):
            v[d] = jnp.where(pop, v[d + 1], v[d])
    return out


def _row_groups(x):
    return [x[V7X_SUBLANES * g:V7X_SUBLANES * (g + 1)] for g in range(x.shape[0] // V7X_SUBLANES)]


def _candidate_groups(a, b):
    sub = lax.broadcasted_iota(jnp.int32, (V7X_SUBLANES, a[0].shape[1]), 0)
    pad = (jnp.full_like(a[0], -jnp.inf), jnp.zeros_like(b[0]))
    rows = [(a[r1], b[r2]) for r1, r2 in _CAND] + [pad] * (N_CAND_ROWS - len(_CAND))
    sums, prods = [], []
    for g in range(N_CAND_ROWS // V7X_SUBLANES):
        grp = rows[V7X_SUBLANES * g:V7X_SUBLANES * (g + 1)]
        ca = jnp.broadcast_to(grp[0][0], sub.shape)
        cb = jnp.broadcast_to(grp[0][1], sub.shape)
        for j in range(1, V7X_SUBLANES):
            if grp[j][0] is not grp[j - 1][0]:
                ca = jnp.where(sub >= j, grp[j][0], ca)
            if grp[j][1] is not grp[j - 1][1]:
                cb = jnp.where(sub >= j, grp[j][1], cb)
        sums.append(ca + cb)
        prods.append(jnp.exp(ca - a[0]) * jnp.exp(cb - b[0]))
    return sums, prods


def _count_above(rows, x):
    n = jnp.zeros(x.shape, F32)
    for r, row in enumerate(rows):
        n = jnp.where(row > x, float(r + 1), n)
    return n


def _route_kernel(hT_ref, wqT_ref, sk_ref, *refs):
    if len(refs) == 5:
        n1_ref, e1_ref, r2_ref, e2_ref, qT_ref = refs
    else:
        u_ref, v_ref, n1_ref, e1_ref, r2_ref, e2_ref, ub_ref, vT_ref, qT_ref = refs
        ub_ref[...] = u_ref[...].astype(BF16)
        vT_ref[...] = v_ref[...].T.astype(BF16)
    qT_ref[...] = _dot(wqT_ref[...], hT_ref[...].astype(BF16))

    def head(h, carry):
        base = pl.multiple_of(h * PEER_QDIM, PEER_QDIM)
        q1 = qT_ref[pl.ds(base, PEER_HALF), :].astype(BF16)
        q2 = qT_ref[pl.ds(base + PEER_HALF, PEER_HALF), :].astype(BF16)
        s1 = _dot(sk_ref[h, 0], q1)
        s2 = _dot(sk_ref[h, 1], q2)
        a = _top_values(_row_groups(s1), N_RANKS)
        b = _top_values(_row_groups(s2), N_RANKS)
        csum, cprod = _candidate_groups(a, b)
        c = _top_values(csum, N_RANKS)
        thr = 0.5 * (c[PEER_TOPK - 1] + c[PEER_TOPK])
        picked = [jnp.where(s >= c[PEER_TOPK - 1], p, 0.0) for s, p in zip(csum, cprod)]
        denom = jnp.sum(functools.reduce(jnp.add, picked), axis=0, keepdims=True)
        split = (PEER_N_KEYS // V7X_SUBLANES, V7X_SUBLANES, s1.shape[1])
        n1_ref[h] = _count_above(b[:PEER_TOPK], thr - s1).reshape(split)
        e1_ref[h] = (jnp.exp(s1 - a[0]) * (1.0 / denom)).reshape(split)
        r2_ref[h] = _count_above(b[:PEER_TOPK], s2).astype(BF16)
        e2_ref[h] = jnp.exp(s2 - b[0]).astype(BF16)
        return carry

    lax.fori_loop(0, PEER_HEADS, head, 0)


PEER_TABLE_ROWS_MAX = 512


def _peer_route(hT, wqT_bf, sub_keys_bf, u_tab, v_tab, tm):
    n_tok = hT.shape[1]
    steps = n_tok // tm
    groups = PEER_N_KEYS // V7X_SUBLANES
    out4 = jax.ShapeDtypeStruct((PEER_HEADS, groups, V7X_SUBLANES, n_tok), F32)
    out3 = jax.ShapeDtypeStruct((PEER_HEADS, PEER_N_KEYS, n_tok), BF16)
    spec4 = pl.BlockSpec((PEER_HEADS, groups, V7X_SUBLANES, tm), lambda i: (0, 0, 0, i))
    spec3 = pl.BlockSpec((PEER_HEADS, PEER_N_KEYS, tm), lambda i: (0, 0, i))
    in_specs = [pl.BlockSpec((D_MODEL, tm), lambda i: (0, i)),
                pl.BlockSpec((PEER_HEADS * PEER_QDIM, D_MODEL), lambda i: (0, 0), pipeline_mode=pl.Buffered(1)),
                pl.BlockSpec((PEER_HEADS, 2, PEER_N_KEYS, PEER_HALF), lambda i: (0, 0, 0, 0))]
    out_specs, out_shape, args = [spec4, spec4, spec3, spec3], [out4, out4, out3, out3], [hT, wqT_bf, sub_keys_bf]
    rows = PEER_N_EXPERTS // steps
    fuse_tables = rows * steps == PEER_N_EXPERTS and rows % V7X_LANES == 0 and rows <= PEER_TABLE_ROWS_MAX
    if fuse_tables:
        in_specs += [pl.BlockSpec((rows, D_MODEL), lambda i: (i, 0))] * 2
        out_specs += [pl.BlockSpec((rows, D_MODEL), lambda i: (i, 0)), pl.BlockSpec((D_MODEL, rows), lambda i: (0, i))]
        out_shape += [jax.ShapeDtypeStruct((PEER_N_EXPERTS, D_MODEL), BF16),
                      jax.ShapeDtypeStruct((D_MODEL, PEER_N_EXPERTS), BF16)]
        args += [u_tab, v_tab]
    outs = pl.pallas_call(
        _route_kernel,
        grid=(steps,),
        in_specs=in_specs,
        out_specs=out_specs,
        out_shape=out_shape,
        scratch_shapes=[pltpu.VMEM((PEER_HEADS * PEER_QDIM, tm), F32)],
        compiler_params=_params("arbitrary"),
        name="peer_route",
    )(*args)
    if not fuse_tables:
        outs = list(outs) + [u_tab.astype(BF16), v_tab.T.astype(BF16)]
    return outs


def _gelu(x):
    return 0.5 * x * (1.0 + lax.erf(x * (2.0 ** -0.5)))


def _rows_bf16(row, n_rows):
    packed_rows = 2 * V7X_SUBLANES
    tile = jnp.broadcast_to(row, (packed_rows, row.shape[1])).astype(BF16)
    return jnp.concatenate([tile] * (n_rows // packed_rows), axis=0)


PEER_OUT_CHUNKS = 1


def _peer_kernel(hT_ref, u_ref, vT_ref, n1_ref, e1_ref, r2_ref, e2_ref, g_ref, b_ref, o_ref,
                 hb_ref, acc_ref):
    j = pl.program_id(1)

    @pl.when(j == 0)
    def _():
        hb_ref[...] = hT_ref[...].astype(BF16)
        acc_ref[...] = jnp.zeros_like(acc_ref)

    act = _gelu(_dot(u_ref[...], hb_ref[...])).astype(BF16)
    slabs = []
    for i1 in range(act.shape[0] // PEER_N_KEYS):
        grp, sub = divmod(i1, V7X_SUBLANES)
        w = None
        for h in range(PEER_HEADS):
            sel = r2_ref[h] < _rows_bf16(n1_ref[h, grp, sub:sub + 1, :], PEER_N_KEYS)
            wh = jnp.where(sel, _rows_bf16(e1_ref[h, grp, sub:sub + 1, :], PEER_N_KEYS) * e2_ref[h], 0.0)
            w = wh if w is None else w + wh
        slabs.append(act[i1 * PEER_N_KEYS:(i1 + 1) * PEER_N_KEYS] * w)
    a = jnp.concatenate(slabs, axis=0)
    rows = acc_ref.shape[0] // PEER_OUT_CHUNKS
    for c in range(PEER_OUT_CHUNKS):
        out = slice(c * rows, (c + 1) * rows)
        acc_ref[out, :] += _dot(vT_ref[out, :], a)

    @pl.when(j == pl.num_programs(1) - 1)
    def _():
        r = (ALPHA * hT_ref[...] + acc_ref[...]).T
        o_ref[...] = _layer_norm(r, g_ref[...], b_ref[...])


def _peer_experts(hT, u_bf, vT_bf, n1, e1, r2, e2, ln_g, ln_b, tm, te):
    n_tok = hT.shape[1]
    groups = te // (PEER_N_KEYS * V7X_SUBLANES)
    assert groups * PEER_N_KEYS * V7X_SUBLANES == te
    tok = lambda i, j: (0, 0, i)
    blk = lambda i, j: (0, j, 0, i)
    return pl.pallas_call(
        _peer_kernel,
        grid=(n_tok // tm, PEER_N_EXPERTS // te),
        in_specs=[pl.BlockSpec((D_MODEL, tm), lambda i, j: (0, i)),
                  pl.BlockSpec((te, D_MODEL), lambda i, j: (j, 0)),
                  pl.BlockSpec((D_MODEL, te), lambda i, j: (0, j)),
                  pl.BlockSpec((PEER_HEADS, groups, V7X_SUBLANES, tm), blk),
                  pl.BlockSpec((PEER_HEADS, groups, V7X_SUBLANES, tm), blk),
                  pl.BlockSpec((PEER_HEADS, PEER_N_KEYS, tm), tok),
                  pl.BlockSpec((PEER_HEADS, PEER_N_KEYS, tm), tok),
                  pl.BlockSpec((1, D_MODEL), lambda i, j: (0, 0)),
                  pl.BlockSpec((1, D_MODEL), lambda i, j: (0, 0))],
        out_specs=pl.BlockSpec((tm, D_MODEL), lambda i, j: (i, 0)),
        out_shape=jax.ShapeDtypeStruct((n_tok, D_MODEL), F32),
        scratch_shapes=[pltpu.VMEM((D_MODEL, tm), BF16), pltpu.VMEM((D_MODEL, tm), F32)],
        compiler_params=_params("arbitrary", "arbitrary"),
        name="peer_experts",
    )(hT, u_bf, vT_bf, n1, e1, r2, e2, ln_g, ln_b)


def _layer(h, batch, seq, w_in, b_gate, sinks, conv_w, w_ao, w_co, w_o, ln1_g, ln1_b,
           w_q, sub_keys, u_tab, v_tab, ln2_g, ln2_b):
    n_tok = h.shape[0]
    t = _tiles(n_tok, seq)
    row = lambda v: v.reshape(1, -1)
    h_bf = h.astype(BF16)
    qkv = _qkv_proj(h_bf, w_in, t["proj"], t["col"])
    attn_o, w_in_bf, w_ao_bf, w_co_bf, w_o_bf = _attention(qkv, sinks, batch, seq, [w_in, w_ao, w_co, w_o])
    conv_o = _conv_branch(h_bf, w_in_bf, conv_w, seq, t["proj"], t["col"])
    hT = _mix_ln1(h_bf, h, attn_o, conv_o, w_in_bf, row(b_gate), w_ao_bf, w_co_bf, w_o_bf,
                  row(ln1_g), row(ln1_b), t["mix"], t["mix_col"])
    n1, e1, r2, e2, u_bf, vT_bf = _peer_route(hT, w_q.T.astype(BF16), sub_keys.astype(BF16), u_tab, v_tab,
                                              t["route"])
    return _peer_experts(hT, u_bf, vT_bf, n1, e1, r2, e2, row(ln2_g), row(ln2_b), t["peer"], t["experts"])


def kernel(x, w_in, b_gate, sinks, conv_w, w_attn_out, w_conv_out, w_o, ln1_g, ln1_b, peer_w_q,
           peer_sub_keys, peer_u, peer_v, ln2_g, ln2_b):
    batch, seq, d_model = x.shape
    assert d_model == D_MODEL and seq % WINDOW == 0
    h = x.reshape(batch * seq, d_model)
    for l in range(w_in.shape[0]):
        h = _layer(h, batch, seq, w_in[l], b_gate[l], sinks[l], conv_w[l], w_attn_out[l],
                   w_conv_out[l], w_o[l], ln1_g[l], ln1_b[l], peer_w_q[l], peer_sub_keys[l],
                   peer_u[l], peer_v[l], ln2_g[l], ln2_b[l])
    return h.reshape(batch, seq, d_model)
```

```python
import functools
import math

import jax
import jax.numpy as jnp
from jax import lax
from jax.experimental import pallas as pl
from jax.experimental.pallas import tpu as pltpu

F32 = jnp.float32
BF16 = jnp.bfloat16

D_MODEL = 2048
N_Q_HEADS = 32
N_KV_HEADS = 4
HEAD_DIM = 64
Q_PER_KV = N_Q_HEADS // N_KV_HEADS
WINDOW = 128
ATTN_WIDTH = N_Q_HEADS * HEAD_DIM
KV_WIDTH = N_KV_HEADS * HEAD_DIM
QKV_WIDTH = ATTN_WIDTH + 2 * KV_WIDTH
CONV_WIDTH = D_MODEL
CONV_K = 3
K_OFF = ATTN_WIDTH
V_OFF = K_OFF + KV_WIDTH
CB_OFF = V_OFF + KV_WIDTH
CC_OFF = CB_OFF + CONV_WIDTH
CH_OFF = CC_OFF + CONV_WIDTH
G_OFF = CH_OFF + CONV_WIDTH

PEER_HEADS = 8
PEER_N_KEYS = 128
PEER_N_EXPERTS = PEER_N_KEYS * PEER_N_KEYS
PEER_TOPK = 16
PEER_HALF = 128
PEER_QDIM = 2 * PEER_HALF

LN_EPS = 1e-5
DEPTH = 1
ALPHA = (2.0 * DEPTH) ** 0.25

V7X_LANES = 128
V7X_SUBLANES = 8
V7X_VMEM_LIMIT_BYTES = 56 * 1024 * 1024

N_RANKS = PEER_TOPK + 1
_CAND = [(r1, r2) for r1 in range(N_RANKS) for r2 in range(N_RANKS) if (r1 + 1) * (r2 + 1) <= N_RANKS]
N_CAND_ROWS = -(-len(_CAND) // V7X_SUBLANES) * V7X_SUBLANES


def _tiles(n_tokens, seq):
    return dict(
        proj=min(1024, seq),
        mix=min(512, n_tokens),
        route=min(256, n_tokens),
        peer=min(512, n_tokens),
        col=512,
        mix_col=256,
        experts=1024,
    )


def _params(*sem):
    return pltpu.CompilerParams(dimension_semantics=sem, vmem_limit_bytes=V7X_VMEM_LIMIT_BYTES)


def _dot(a, b):
    return jnp.dot(a, b, preferred_element_type=F32)


def _qkv_kernel(x_ref, w_ref, o_ref, wb_ref):
    @pl.when(pl.program_id(1) == 0)
    def _():
        wb_ref[...] = w_ref[...].astype(BF16)

    o_ref[...] = _dot(x_ref[...], wb_ref[...]).astype(o_ref.dtype)


def _qkv_proj(x_bf, w_in, tm, tn):
    n_tok = x_bf.shape[0]
    return pl.pallas_call(
        _qkv_kernel,
        grid=(QKV_WIDTH // tn, n_tok // tm),
        in_specs=[pl.BlockSpec((tm, D_MODEL), lambda j, i: (i, 0)),
                  pl.BlockSpec((D_MODEL, tn), lambda j, i: (0, j))],
        out_specs=pl.BlockSpec((tm, tn), lambda j, i: (i, j)),
        out_shape=jax.ShapeDtypeStruct((n_tok, QKV_WIDTH), BF16),
        scratch_shapes=[pltpu.VMEM((D_MODEL, tn), BF16)],
        compiler_params=_params("arbitrary", "arbitrary"),
        name="qkv_proj",
    )(x_bf, w_in)


def _conv_kernel(x_ref, wcb_ref, wcc_ref, wch_ref, cw_ref, o_ref, carry_ref, *, tiles_per_seq):
    i = pl.program_id(1)
    x = x_ref[...]
    cb = _dot(x, wcb_ref[...])
    u = _dot(x, wcc_ref[...]) * _dot(x, wch_ref[...])
    tm = u.shape[0]

    @pl.when(i % tiles_per_seq == 0)
    def _():
        carry_ref[...] = jnp.zeros_like(carry_ref)

    prev = carry_ref[...]
    row = lax.broadcasted_iota(jnp.int32, u.shape, 0)
    u1 = jnp.where(row == 0, prev[7:8], pltpu.roll(u, 1, 0))
    u2 = pltpu.roll(u, 2, 0)
    u2 = jnp.where(row == 0, prev[6:7], jnp.where(row == 1, prev[7:8], u2))
    cw = cw_ref[...]
    y = cb * (cw[0:1] * u2 + cw[1:2] * u1 + cw[2:3] * u)
    o_ref[...] = y.astype(o_ref.dtype)
    carry_ref[...] = u[tm - V7X_SUBLANES:tm]


def _conv_branch(x_bf, w_in_bf, conv_w, seq, tm, nc):
    n_tok = x_bf.shape[0]
    wspec = lambda off: pl.BlockSpec((D_MODEL, nc), lambda j, i, o=off // nc: (0, o + j))
    return pl.pallas_call(
        functools.partial(_conv_kernel, tiles_per_seq=seq // tm),
        grid=(CONV_WIDTH // nc, n_tok // tm),
        in_specs=[pl.BlockSpec((tm, D_MODEL), lambda j, i: (i, 0)),
                  wspec(CB_OFF), wspec(CC_OFF), wspec(CH_OFF),
                  pl.BlockSpec((CONV_K, nc), lambda j, i: (0, j))],
        out_specs=pl.BlockSpec((tm, nc), lambda j, i: (i, j)),
        out_shape=jax.ShapeDtypeStruct((n_tok, CONV_WIDTH), BF16),
        scratch_shapes=[pltpu.VMEM((V7X_SUBLANES, nc), F32)],
        compiler_params=_params("arbitrary", "arbitrary"),
        name="conv_branch",
    )(x_bf, w_in_bf, w_in_bf, w_in_bf, conv_w)


def _attn_kernel(sinks_ref, q_ref, kc_ref, kp_ref, vc_ref, vp_ref, *refs):
    n_side = (len(refs) - 1) // 2
    o_ref = refs[n_side]
    for src, dst in zip(refs[:n_side], refs[n_side + 1:]):
        dst[...] = src[...].astype(BF16)
    n = pl.program_id(1)
    cols = Q_PER_KV * WINDOW
    kj = lax.broadcasted_iota(jnp.int32, (2 * WINDOW, cols), 0)
    qi = lax.broadcasted_iota(jnp.int32, (2 * WINDOW, cols), 1) & (WINDOW - 1)
    rel = WINDOW + qi - kj
    mask = (rel >= 0) & (rel < WINDOW) & ((kj >= WINDOW) | (n > 0))
    scale = HEAD_DIM ** -0.5
    for j in range(N_KV_HEADS):
        heads = [j * Q_PER_KV + g for g in range(Q_PER_KV)]
        qg = jnp.concatenate([q_ref[:, h * HEAD_DIM:(h + 1) * HEAD_DIM] for h in heads], axis=0)
        ksl = slice(j * HEAD_DIM, (j + 1) * HEAD_DIM)
        kw = jnp.concatenate([kp_ref[:, ksl], kc_ref[:, ksl]], axis=0)
        vw = jnp.concatenate([vp_ref[:, ksl], vc_ref[:, ksl]], axis=0)
        s = lax.dot_general(kw, qg * scale, (((1,), (1,)), ((), ())), preferred_element_type=F32)
        s = jnp.where(mask, s, -jnp.inf)
        sink = jnp.concatenate([jnp.full((1, WINDOW), sinks_ref[h], F32) for h in heads], axis=1)
        m = jnp.maximum(jnp.max(s, axis=0, keepdims=True), sink)
        p = jnp.exp(s - m)
        denom = jnp.sum(p, axis=0, keepdims=True) + jnp.exp(sink - m)
        probs = (p / denom).astype(BF16)
        o = lax.dot_general(probs, vw, (((0,), (0,)), ((), ())), preferred_element_type=F32)
        for g, h in enumerate(heads):
            o_ref[:, h * HEAD_DIM:(h + 1) * HEAD_DIM] = o[g * WINDOW:(g + 1) * WINDOW].astype(o_ref.dtype)


ATTN_SIDE_ROWS_MAX = 64


def _attention(qkv, sinks, batch, seq, weights):
    n_tok = qkv.shape[0]
    nb = seq // WINDOW
    cur = lambda b, n: b * nb + n
    prev = lambda b, n: b * nb + jnp.maximum(n - 1, 0)
    kcol, vcol = K_OFF // KV_WIDTH, V_OFF // KV_WIDTH
    rows = D_MODEL // (batch * nb)
    fuse = (rows * batch * nb == D_MODEL and rows % (2 * V7X_SUBLANES) == 0 and rows <= ATTN_SIDE_ROWS_MAX)
    side = weights if fuse else []
    wspec = lambda w: pl.BlockSpec((rows, w.shape[1]), lambda b, n: (cur(b, n), 0))
    outs = pl.pallas_call(
        _attn_kernel,
        grid=(batch, nb),
        in_specs=[pl.BlockSpec(memory_space=pltpu.SMEM),
                  pl.BlockSpec((WINDOW, ATTN_WIDTH), lambda b, n: (cur(b, n), 0)),
                  pl.BlockSpec((WINDOW, KV_WIDTH), lambda b, n: (cur(b, n), kcol)),
                  pl.BlockSpec((WINDOW, KV_WIDTH), lambda b, n: (prev(b, n), kcol)),
                  pl.BlockSpec((WINDOW, KV_WIDTH), lambda b, n: (cur(b, n), vcol)),
                  pl.BlockSpec((WINDOW, KV_WIDTH), lambda b, n: (prev(b, n), vcol))]
        + [wspec(w) for w in side],
        out_specs=[pl.BlockSpec((WINDOW, ATTN_WIDTH), lambda b, n: (cur(b, n), 0))] + [wspec(w) for w in side],
        out_shape=[jax.ShapeDtypeStruct((n_tok, ATTN_WIDTH), BF16)]
        + [jax.ShapeDtypeStruct(w.shape, BF16) for w in side],
        compiler_params=_params("arbitrary", "arbitrary"),
        name="swa_attention",
    )(sinks, qkv, qkv, qkv, qkv, qkv, *side)
    if not fuse:
        outs = list(outs) + [w.astype(BF16) for w in weights]
    return outs


def _layer_norm(r, g, b):
    mu = jnp.mean(r, axis=-1, keepdims=True)
    d = r - mu
    var = jnp.mean(d * d, axis=-1, keepdims=True)
    return d * lax.rsqrt(var + LN_EPS) * g + b


def _sigmoid(z):
    return 1.0 / (1.0 + jnp.exp(-z))


def _mix_kernel(xb_ref, xf_ref, a_ref, c_ref, wga_ref, wgc_ref, bga_ref, bgc_ref, wao_ref, wco_ref,
                wo_ref, g_ref, b_ref, hT_ref, acc_ref):
    j = pl.program_id(1)

    @pl.when(j == 0)
    def _():
        acc_ref[...] = jnp.zeros_like(acc_ref)

    xb = xb_ref[...]
    ga = _sigmoid(_dot(xb, wga_ref[...]) + bga_ref[...])
    gc = _sigmoid(_dot(xb, wgc_ref[...]) + bgc_ref[...])
    merged = ga * _dot(a_ref[...], wao_ref[...]) + gc * _dot(c_ref[...], wco_ref[...])
    acc_ref[...] += _dot(merged.astype(BF16), wo_ref[...])

    @pl.when(j == pl.num_programs(1) - 1)
    def _():
        h1 = _layer_norm(ALPHA * xf_ref[...] + acc_ref[...], g_ref[...], b_ref[...])
        hT_ref[...] = h1.T


def _mix_ln1(x_bf, x_f32, attn_o, conv_o, w_in_bf, b_gate, w_ao, w_co, w_o, ln_g, ln_b, tm, nc):
    n_tok = x_bf.shape[0]
    row = lambda i, j: (i, 0)
    gcol = G_OFF // nc
    return pl.pallas_call(
        _mix_kernel,
        grid=(n_tok // tm, D_MODEL // nc),
        in_specs=[pl.BlockSpec((tm, D_MODEL), row), pl.BlockSpec((tm, D_MODEL), row),
                  pl.BlockSpec((tm, ATTN_WIDTH), row), pl.BlockSpec((tm, CONV_WIDTH), row),
                  pl.BlockSpec((D_MODEL, nc), lambda i, j: (0, gcol + j)),
                  pl.BlockSpec((D_MODEL, nc), lambda i, j: (0, gcol + D_MODEL // nc + j)),
                  pl.BlockSpec((1, nc), lambda i, j: (0, j)),
                  pl.BlockSpec((1, nc), lambda i, j: (0, D_MODEL // nc + j)),
                  pl.BlockSpec((ATTN_WIDTH, nc), lambda i, j: (0, j)),
                  pl.BlockSpec((CONV_WIDTH, nc), lambda i, j: (0, j)),
                  pl.BlockSpec((nc, D_MODEL), lambda i, j: (j, 0)),
                  pl.BlockSpec((1, D_MODEL), lambda i, j: (0, 0)),
                  pl.BlockSpec((1, D_MODEL), lambda i, j: (0, 0))],
        out_specs=pl.BlockSpec((D_MODEL, tm), lambda i, j: (0, i)),
        out_shape=jax.ShapeDtypeStruct((D_MODEL, n_tok), F32),
        scratch_shapes=[pltpu.VMEM((tm, D_MODEL), F32)],
        compiler_params=_params("arbitrary", "arbitrary"),
        name="mix_ln1",
    )(x_bf, x_f32, attn_o, conv_o, w_in_bf, w_in_bf, b_gate, b_gate, w_ao, w_co, w_o, ln_g, ln_b)


def _merge_exchange(n):
    pairs = []
    t = max(1, math.ceil(math.log2(n)))
    p = 1 << (t - 1)
    while p > 0:
        q, r, d = 1 << (t - 1), 0, p
        while d > 0:
            pairs += [(i, i + d) for i in range(n - d) if (i & p) == r]
            d, q, r = q - p, q >> 1, p
        p >>= 1
    return pairs


def _top_values(groups, k):
    v = list(groups)
    for i, j in _merge_exchange(len(v)):
        v[i], v[j] = jnp.maximum(v[i], v[j]), jnp.minimum(v[i], v[j])
    v.append(jnp.full(v[0].shape, -jnp.inf, F32))
    out = []
    for r in range(k):
        m = jnp.max(v[0], axis=0, keepdims=True)
        out.append(m)
        pop = v[0] == m
        for d in range(min(len(v) - 1, k - 1 - r)):
            v[d] = jnp.where(pop, v[d + 1], v[d])
    return out


def _row_groups(x):
    return [x[V7X_SUBLANES * g:V7X_SUBLANES * (g + 1)] for g in range(x.shape[0] // V7X_SUBLANES)]


def _candidate_groups(a, b):
    sub = lax.broadcasted_iota(jnp.int32, (V7X_SUBLANES, a[0].shape[1]), 0)
    pad = (jnp.full_like(a[0], -jnp.inf), jnp.zeros_like(b[0]))
    rows = [(a[r1], b[r2]) for r1, r2 in _CAND] + [pad] * (N_CAND_ROWS - len(_CAND))
    sums, prods = [], []
    for g in range(N_CAND_ROWS // V7X_SUBLANES):
        grp = rows[V7X_SUBLANES * g:V7X_SUBLANES * (g + 1)]
        ca = jnp.broadcast_to(grp[0][0], sub.shape)
        cb = jnp.broadcast_to(grp[0][1], sub.shape)
        for j in range(1, V7X_SUBLANES):
            if grp[j][0] is not grp[j - 1][0]:
                ca = jnp.where(sub >= j, grp[j][0], ca)
            if grp[j][1] is not grp[j - 1][1]:
                cb = jnp.where(sub >= j, grp[j][1], cb)
        sums.append(ca + cb)
        prods.append(jnp.exp(ca - a[0]) * jnp.exp(cb - b[0]))
    return sums, prods


def _count_above(rows, x):
    n = jnp.zeros(x.shape, F32)
    for r, row in enumerate(rows):
        n = jnp.where(row > x, float(r + 1), n)
    return n


def _route_kernel(hT_ref, wqT_ref, sk_ref, *refs):
    if len(refs) == 5:
        n1_ref, e1_ref, r2_ref, e2_ref, qT_ref = refs
    else:
        u_ref, v_ref, n1_ref, e1_ref, r2_ref, e2_ref, ub_ref, vT_ref, qT_ref = refs
        ub_ref[...] = u_ref[...].astype(BF16)
        vT_ref[...] = v_ref[...].T.astype(BF16)
    qT_ref[...] = _dot(wqT_ref[...], hT_ref[...].astype(BF16))

    def head(h, carry):
        base = pl.multiple_of(h * PEER_QDIM, PEER_QDIM)
        q1 = qT_ref[pl.ds(base, PEER_HALF), :].astype(BF16)
        q2 = qT_ref[pl.ds(base + PEER_HALF, PEER_HALF), :].astype(BF16)
        s1 = _dot(sk_ref[h, 0], q1)
        s2 = _dot(sk_ref[h, 1], q2)
        a = _top_values(_row_groups(s1), N_RANKS)
        b = _top_values(_row_groups(s2), N_RANKS)
        csum, cprod = _candidate_groups(a, b)
        c = _top_values(csum, N_RANKS)
        thr = 0.5 * (c[PEER_TOPK - 1] + c[PEER_TOPK])
        picked = [jnp.where(s >= c[PEER_TOPK - 1], p, 0.0) for s, p in zip(csum, cprod)]
        denom = jnp.sum(functools.reduce(jnp.add, picked), axis=0, keepdims=True)
        split = (PEER_N_KEYS // V7X_SUBLANES, V7X_SUBLANES, s1.shape[1])
        n1_ref[h] = _count_above(b[:PEER_TOPK], thr - s1).reshape(split)
        e1_ref[h] = (jnp.exp(s1 - a[0]) * (1.0 / denom)).reshape(split)
        r2_ref[h] = _count_above(b[:PEER_TOPK], s2).astype(BF16)
        e2_ref[h] = jnp.exp(s2 - b[0]).astype(BF16)
        return carry

    lax.fori_loop(0, PEER_HEADS, head, 0)


PEER_TABLE_ROWS_MAX = 512


def _peer_route(hT, wqT_bf, sub_keys_bf, u_tab, v_tab, tm):
    n_tok = hT.shape[1]
    steps = n_tok // tm
    groups = PEER_N_KEYS // V7X_SUBLANES
    out4 = jax.ShapeDtypeStruct((PEER_HEADS, groups, V7X_SUBLANES, n_tok), F32)
    out3 = jax.ShapeDtypeStruct((PEER_HEADS, PEER_N_KEYS, n_tok), BF16)
    spec4 = pl.BlockSpec((PEER_HEADS, groups, V7X_SUBLANES, tm), lambda i: (0, 0, 0, i))
    spec3 = pl.BlockSpec((PEER_HEADS, PEER_N_KEYS, tm), lambda i: (0, 0, i))
    in_specs = [pl.BlockSpec((D_MODEL, tm), lambda i: (0, i)),
                pl.BlockSpec((PEER_HEADS * PEER_QDIM, D_MODEL), lambda i: (0, 0), pipeline_mode=pl.Buffered(1)),
                pl.BlockSpec((PEER_HEADS, 2, PEER_N_KEYS, PEER_HALF), lambda i: (0, 0, 0, 0))]
    out_specs, out_shape, args = [spec4, spec4, spec3, spec3], [out4, out4, out3, out3], [hT, wqT_bf, sub_keys_bf]
    rows = PEER_N_EXPERTS // steps
    fuse_tables = rows * steps == PEER_N_EXPERTS and rows % V7X_LANES == 0 and rows <= PEER_TABLE_ROWS_MAX
    if fuse_tables:
        in_specs += [pl.BlockSpec((rows, D_MODEL), lambda i: (i, 0))] * 2
        out_specs += [pl.BlockSpec((rows, D_MODEL), lambda i: (i, 0)), pl.BlockSpec((D_MODEL, rows), lambda i: (0, i))]
        out_shape += [jax.ShapeDtypeStruct((PEER_N_EXPERTS, D_MODEL), BF16),
                      jax.ShapeDtypeStruct((D_MODEL, PEER_N_EXPERTS), BF16)]
        args += [u_tab, v_tab]
    outs = pl.pallas_call(
        _route_kernel,
        grid=(steps,),
        in_specs=in_specs,
        out_specs=out_specs,
        out_shape=out_shape,
        scratch_shapes=[pltpu.VMEM((PEER_HEADS * PEER_QDIM, tm), F32)],
        compiler_params=_params("arbitrary"),
        name="peer_route",
    )(*args)
    if not fuse_tables:
        outs = list(outs) + [u_tab.astype(BF16), v_tab.T.astype(BF16)]
    return outs


def _gelu(x):
    return 0.5 * x * (1.0 + lax.erf(x * (2.0 ** -0.5)))


def _rows_bf16(row, n_rows):
    packed_rows = 2 * V7X_SUBLANES
    tile = jnp.broadcast_to(row, (packed_rows, row.shape[1])).astype(BF16)
    return jnp.concatenate([tile] * (n_rows // packed_rows), axis=0)


def _peer_kernel(hT_ref, u_ref, vT_ref, n1_ref, e1_ref, r2_ref, e2_ref, g_ref, b_ref, o_ref,
                 hb_ref, acc_ref):
    j = pl.program_id(1)

    @pl.when(j == 0)
    def _():
        hb_ref[...] = hT_ref[...].astype(BF16)
        acc_ref[...] = jnp.zeros_like(acc_ref)

    act = _gelu(_dot(u_ref[...], hb_ref[...]).astype(BF16))
    slabs = []
    for i1 in range(act.shape[0] // PEER_N_KEYS):
        grp, sub = divmod(i1, V7X_SUBLANES)
        w = None
        for h in range(PEER_HEADS):
            sel = r2_ref[h] < _rows_bf16(n1_ref[h, grp, sub:sub + 1, :], PEER_N_KEYS)
            wh = jnp.where(sel, _rows_bf16(e1_ref[h, grp, sub:sub + 1, :], PEER_N_KEYS) * e2_ref[h], 0.0)
            w = wh if w is None else w + wh
        slabs.append(act[i1 * PEER_N_KEYS:(i1 + 1) * PEER_N_KEYS] * w)
    acc_ref[...] += _dot(vT_ref[...], jnp.concatenate(slabs, axis=0))

    @pl.when(j == pl.num_programs(1) - 1)
    def _():
        r = (ALPHA * hT_ref[...] + acc_ref[...]).T
        o_ref[...] = _layer_norm(r, g_ref[...], b_ref[...])


def _peer_experts(hT, u_bf, vT_bf, n1, e1, r2, e2, ln_g, ln_b, tm, te):
    n_tok = hT.shape[1]
    groups = te // (PEER_N_KEYS * V7X_SUBLANES)
    assert groups * PEER_N_KEYS * V7X_SUBLANES == te
    tok = lambda i, j: (0, 0, i)
    blk = lambda i, j: (0, j, 0, i)
    return pl.pallas_call(
        _peer_kernel,
        grid=(n_tok // tm, PEER_N_EXPERTS // te),
        in_specs=[pl.BlockSpec((D_MODEL, tm), lambda i, j: (0, i)),
                  pl.BlockSpec((te, D_MODEL), lambda i, j: (j, 0)),
                  pl.BlockSpec((D_MODEL, te), lambda i, j: (0, j)),
                  pl.BlockSpec((PEER_HEADS, groups, V7X_SUBLANES, tm), blk),
                  pl.BlockSpec((PEER_HEADS, groups, V7X_SUBLANES, tm), blk),
                  pl.BlockSpec((PEER_HEADS, PEER_N_KEYS, tm), tok),
                  pl.BlockSpec((PEER_HEADS, PEER_N_KEYS, tm), tok),
                  pl.BlockSpec((1, D_MODEL), lambda i, j: (0, 0)),
                  pl.BlockSpec((1, D_MODEL), lambda i, j: (0, 0))],
        out_specs=pl.BlockSpec((tm, D_MODEL), lambda i, j: (i, 0)),
        out_shape=jax.ShapeDtypeStruct((n_tok, D_MODEL), F32),
        scratch_shapes=[pltpu.VMEM((D_MODEL, tm), BF16), pltpu.VMEM((D_MODEL, tm), F32)],
        compiler_params=_params("arbitrary", "arbitrary"),
        name="peer_experts",
    )(hT, u_bf, vT_bf, n1, e1, r2, e2, ln_g, ln_b)


def _layer(h, batch, seq, w_in, b_gate, sinks, conv_w, w_ao, w_co, w_o, ln1_g, ln1_b,
           w_q, sub_keys, u_tab, v_tab, ln2_g, ln2_b):
    n_tok = h.shape[0]
    t = _tiles(n_tok, seq)
    row = lambda v: v.reshape(1, -1)
    h_bf = h.astype(BF16)
    qkv = _qkv_proj(h_bf, w_in, t["proj"], t["col"])
    attn_o, w_in_bf, w_ao_bf, w_co_bf, w_o_bf = _attention(qkv, sinks, batch, seq, [w_in, w_ao, w_co, w_o])
    conv_o = _conv_branch(h_bf, w_in_bf, conv_w, seq, t["proj"], t["col"])
    hT = _mix_ln1(h_bf, h, attn_o, conv_o, w_in_bf, row(b_gate), w_ao_bf, w_co_bf, w_o_bf,
                  row(ln1_g), row(ln1_b), t["mix"], t["mix_col"])
    n1, e1, r2, e2, u_bf, vT_bf = _peer_route(hT, w_q.T.astype(BF16), sub_keys.astype(BF16), u_tab, v_tab,
                                              t["route"])
    return _peer_experts(hT, u_bf, vT_bf, n1, e1, r2, e2, row(ln2_g), row(ln2_b), t["peer"], t["experts"])


def kernel(x, w_in, b_gate, sinks, conv_w, w_attn_out, w_conv_out, w_o, ln1_g, ln1_b, peer_w_q,
           peer_sub_keys, peer_u, peer_v, ln2_g, ln2_b):
    batch, seq, d_model = x.shape
    assert d_model == D_MODEL and seq % WINDOW == 0
    h = x.reshape(batch * seq, d_model)
    for l in range(w_in.shape[0]):
        h = _layer(h, batch, seq, w_in[l], b_gate[l], sinks[l], conv_w[l], w_attn_out[l],
                   w_conv_out[l], w_o[l], ln1_g[l], ln1_b[l], peer_w_q[l], peer_sub_keys[l],
                   peer_u[l], peer_v[l], ln2_g[l], ln2_b[l])
    return h.reshape(batch, seq, d_model)
```

```python
import functools
import math

import jax
import jax.numpy as jnp
from jax import lax
from jax.experimental import pallas as pl
from jax.experimental.pallas import tpu as pltpu

F32 = jnp.float32
BF16 = jnp.bfloat16

D_MODEL = 2048
N_Q_HEADS = 32
N_KV_HEADS = 4
HEAD_DIM = 64
Q_PER_KV = N_Q_HEADS // N_KV_HEADS
WINDOW = 128
ATTN_WIDTH = N_Q_HEADS * HEAD_DIM
KV_WIDTH = N_KV_HEADS * HEAD_DIM
QKV_WIDTH = ATTN_WIDTH + 2 * KV_WIDTH
CONV_WIDTH = D_MODEL
CONV_K = 3
K_OFF = ATTN_WIDTH
V_OFF = K_OFF + KV_WIDTH
CB_OFF = V_OFF + KV_WIDTH
CC_OFF = CB_OFF + CONV_WIDTH
CH_OFF = CC_OFF + CONV_WIDTH
G_OFF = CH_OFF + CONV_WIDTH

PEER_HEADS = 8
PEER_N_KEYS = 128
PEER_N_EXPERTS = PEER_N_KEYS * PEER_N_KEYS
PEER_TOPK = 16
PEER_HALF = 128
PEER_QDIM = 2 * PEER_HALF

LN_EPS = 1e-5
DEPTH = 1
ALPHA = (2.0 * DEPTH) ** 0.25

V7X_LANES = 128
V7X_SUBLANES = 8
V7X_VMEM_LIMIT_BYTES = 56 * 1024 * 1024

N_RANKS = PEER_TOPK + 1
_CAND = [(r1, r2) for r1 in range(N_RANKS) for r2 in range(N_RANKS) if (r1 + 1) * (r2 + 1) <= N_RANKS]
N_CAND_ROWS = -(-len(_CAND) // V7X_SUBLANES) * V7X_SUBLANES


def _tiles(n_tokens, seq):
    return dict(
        proj=min(1024, seq),
        mix=min(512, n_tokens),
        route=min(256, n_tokens),
        peer=min(512, n_tokens),
        col=512,
        mix_col=256,
        experts=1024,
    )


def _params(*sem):
    return pltpu.CompilerParams(dimension_semantics=sem, vmem_limit_bytes=V7X_VMEM_LIMIT_BYTES)


def _dot(a, b):
    return jnp.dot(a, b, preferred_element_type=F32)


def _qkv_kernel(x_ref, w_ref, o_ref, wb_ref):
    @pl.when(pl.program_id(1) == 0)
    def _():
        wb_ref[...] = w_ref[...].astype(BF16)

    o_ref[...] = _dot(x_ref[...], wb_ref[...]).astype(o_ref.dtype)


def _qkv_proj(x_bf, w_in, tm, tn):
    n_tok = x_bf.shape[0]
    return pl.pallas_call(
        _qkv_kernel,
        grid=(QKV_WIDTH // tn, n_tok // tm),
        in_specs=[pl.BlockSpec((tm, D_MODEL), lambda j, i: (i, 0)),
                  pl.BlockSpec((D_MODEL, tn), lambda j, i: (0, j))],
        out_specs=pl.BlockSpec((tm, tn), lambda j, i: (i, j)),
        out_shape=jax.ShapeDtypeStruct((n_tok, QKV_WIDTH), BF16),
        scratch_shapes=[pltpu.VMEM((D_MODEL, tn), BF16)],
        compiler_params=_params("arbitrary", "arbitrary"),
        name="qkv_proj",
    )(x_bf, w_in)


def _conv_kernel(x_ref, wcb_ref, wcc_ref, wch_ref, cw_ref, o_ref, carry_ref, *, tiles_per_seq):
    i = pl.program_id(1)
    x = x_ref[...]
    cb = _dot(x, wcb_ref[...])
    u = _dot(x, wcc_ref[...]) * _dot(x, wch_ref[...])
    tm = u.shape[0]

    @pl.when(i % tiles_per_seq == 0)
    def _():
        carry_ref[...] = jnp.zeros_like(carry_ref)

    prev = carry_ref[...]
    row = lax.broadcasted_iota(jnp.int32, u.shape, 0)
    u1 = jnp.where(row == 0, prev[7:8], pltpu.roll(u, 1, 0))
    u2 = pltpu.roll(u, 2, 0)
    u2 = jnp.where(row == 0, prev[6:7], jnp.where(row == 1, prev[7:8], u2))
    cw = cw_ref[...]
    y = cb * (cw[0:1] * u2 + cw[1:2] * u1 + cw[2:3] * u)
    o_ref[...] = y.astype(o_ref.dtype)
    carry_ref[...] = u[tm - V7X_SUBLANES:tm]


def _conv_branch(x_bf, w_in_bf, conv_w, seq, tm, nc):
    n_tok = x_bf.shape[0]
    wspec = lambda off: pl.BlockSpec((D_MODEL, nc), lambda j, i, o=off // nc: (0, o + j))
    return pl.pallas_call(
        functools.partial(_conv_kernel, tiles_per_seq=seq // tm),
        grid=(CONV_WIDTH // nc, n_tok // tm),
        in_specs=[pl.BlockSpec((tm, D_MODEL), lambda j, i: (i, 0)),
                  wspec(CB_OFF), wspec(CC_OFF), wspec(CH_OFF),
                  pl.BlockSpec((CONV_K, nc), lambda j, i: (0, j))],
        out_specs=pl.BlockSpec((tm, nc), lambda j, i: (i, j)),
        out_shape=jax.ShapeDtypeStruct((n_tok, CONV_WIDTH), BF16),
        scratch_shapes=[pltpu.VMEM((V7X_SUBLANES, nc), F32)],
        compiler_params=_params("arbitrary", "arbitrary"),
        name="conv_branch",
    )(x_bf, w_in_bf, w_in_bf, w_in_bf, conv_w)


def _attn_kernel(sinks_ref, q_ref, kc_ref, kp_ref, vc_ref, vp_ref, *refs):
    n_side = (len(refs) - 1) // 2
    o_ref = refs[n_side]
    for src, dst in zip(refs[:n_side], refs[n_side + 1:]):
        dst[...] = src[...].astype(BF16)
    n = pl.program_id(1)
    cols = Q_PER_KV * WINDOW
    kj = lax.broadcasted_iota(jnp.int32, (2 * WINDOW, cols), 0)
    qi = lax.broadcasted_iota(jnp.int32, (2 * WINDOW, cols), 1) & (WINDOW - 1)
    rel = WINDOW + qi - kj
    mask = (rel >= 0) & (rel < WINDOW) & ((kj >= WINDOW) | (n > 0))
    scale = HEAD_DIM ** -0.5
    for j in range(N_KV_HEADS):
        heads = [j * Q_PER_KV + g for g in range(Q_PER_KV)]
        qg = jnp.concatenate([q_ref[:, h * HEAD_DIM:(h + 1) * HEAD_DIM] for h in heads], axis=0)
        ksl = slice(j * HEAD_DIM, (j + 1) * HEAD_DIM)
        kw = jnp.concatenate([kp_ref[:, ksl], kc_ref[:, ksl]], axis=0)
        vw = jnp.concatenate([vp_ref[:, ksl], vc_ref[:, ksl]], axis=0)
        s = lax.dot_general(kw, qg * scale, (((1,), (1,)), ((), ())), preferred_element_type=F32)
        s = jnp.where(mask, s, -jnp.inf)
        sink = jnp.concatenate([jnp.full((1, WINDOW), sinks_ref[h], F32) for h in heads], axis=1)
        m = jnp.maximum(jnp.max(s, axis=0, keepdims=True), sink)
        p = jnp.exp(s - m)
        denom = jnp.sum(p, axis=0, keepdims=True) + jnp.exp(sink - m)
        probs = (p / denom).astype(BF16)
        o = lax.dot_general(probs, vw, (((0,), (0,)), ((), ())), preferred_element_type=F32)
        for g, h in enumerate(heads):
            o_ref[:, h * HEAD_DIM:(h + 1) * HEAD_DIM] = o[g * WINDOW:(g + 1) * WINDOW].astype(o_ref.dtype)


ATTN_SIDE_ROWS_MAX = 64


def _attention(qkv, sinks, batch, seq, weights):
    n_tok = qkv.shape[0]
    nb = seq // WINDOW
    cur = lambda b, n: b * nb + n
    prev = lambda b, n: b * nb + jnp.maximum(n - 1, 0)
    kcol, vcol = K_OFF // KV_WIDTH, V_OFF // KV_WIDTH
    rows = D_MODEL // (batch * nb)
    fuse = (rows * batch * nb == D_MODEL and rows % (2 * V7X_SUBLANES) == 0 and rows <= ATTN_SIDE_ROWS_MAX)
    side = weights if fuse else []
    wspec = lambda w: pl.BlockSpec((rows, w.shape[1]), lambda b, n: (cur(b, n), 0))
    outs = pl.pallas_call(
        _attn_kernel,
        grid=(batch, nb),
        in_specs=[pl.BlockSpec(memory_space=pltpu.SMEM),
                  pl.BlockSpec((WINDOW, ATTN_WIDTH), lambda b, n: (cur(b, n), 0)),
                  pl.BlockSpec((WINDOW, KV_WIDTH), lambda b, n: (cur(b, n), kcol)),
                  pl.BlockSpec((WINDOW, KV_WIDTH), lambda b, n: (prev(b, n), kcol)),
                  pl.BlockSpec((WINDOW, KV_WIDTH), lambda b, n: (cur(b, n), vcol)),
                  pl.BlockSpec((WINDOW, KV_WIDTH), lambda b, n: (prev(b, n), vcol))]
        + [wspec(w) for w in side],
        out_specs=[pl.BlockSpec((WINDOW, ATTN_WIDTH), lambda b, n: (cur(b, n), 0))] + [wspec(w) for w in side],
        out_shape=[jax.ShapeDtypeStruct((n_tok, ATTN_WIDTH), BF16)]
        + [jax.ShapeDtypeStruct(w.shape, BF16) for w in side],
        compiler_params=_params("arbitrary", "arbitrary"),
        name="swa_attention",
    )(sinks, qkv, qkv, qkv, qkv, qkv, *side)
    if not fuse:
        outs = list(outs) + [w.astype(BF16) for w in weights]
    return outs


def _layer_norm(r, g, b):
    mu = jnp.mean(r, axis=-1, keepdims=True)
    d = r - mu
    var = jnp.mean(d * d, axis=-1, keepdims=True)
    return d * lax.rsqrt(var + LN_EPS) * g + b


def _sigmoid(z):
    return 1.0 / (1.0 + jnp.exp(-z))


def _mix_kernel(xb_ref, xf_ref, a_ref, c_ref, wga_ref, wgc_ref, bga_ref, bgc_ref, wao_ref, wco_ref,
                wo_ref, g_ref, b_ref, hT_ref, acc_ref):
    j = pl.program_id(1)

    @pl.when(j == 0)
    def _():
        acc_ref[...] = jnp.zeros_like(acc_ref)

    xb = xb_ref[...]
    ga = _sigmoid(_dot(xb, wga_ref[...]) + bga_ref[...])
    gc = _sigmoid(_dot(xb, wgc_ref[...]) + bgc_ref[...])
    merged = ga * _dot(a_ref[...], wao_ref[...]) + gc * _dot(c_ref[...], wco_ref[...])
    acc_ref[...] += _dot(merged.astype(BF16), wo_ref[...])

    @pl.when(j == pl.num_programs(1) - 1)
    def _():
        h1 = _layer_norm(ALPHA * xf_ref[...] + acc_ref[...], g_ref[...], b_ref[...])
        hT_ref[...] = h1.T


def _mix_ln1(x_bf, x_f32, attn_o, conv_o, w_in_bf, b_gate, w_ao, w_co, w_o, ln_g, ln_b, tm, nc):
    n_tok = x_bf.shape[0]
    row = lambda i, j: (i, 0)
    gcol = G_OFF // nc
    return pl.pallas_call(
        _mix_kernel,
        grid=(n_tok // tm, D_MODEL // nc),
        in_specs=[pl.BlockSpec((tm, D_MODEL), row), pl.BlockSpec((tm, D_MODEL), row),
                  pl.BlockSpec((tm, ATTN_WIDTH), row), pl.BlockSpec((tm, CONV_WIDTH), row),
                  pl.BlockSpec((D_MODEL, nc), lambda i, j: (0, gcol + j)),
                  pl.BlockSpec((D_MODEL, nc), lambda i, j: (0, gcol + D_MODEL // nc + j)),
                  pl.BlockSpec((1, nc), lambda i, j: (0, j)),
                  pl.BlockSpec((1, nc), lambda i, j: (0, D_MODEL // nc + j)),
                  pl.BlockSpec((ATTN_WIDTH, nc), lambda i, j: (0, j)),
                  pl.BlockSpec((CONV_WIDTH, nc), lambda i, j: (0, j)),
                  pl.BlockSpec((nc, D_MODEL), lambda i, j: (j, 0)),
                  pl.BlockSpec((1, D_MODEL), lambda i, j: (0, 0)),
                  pl.BlockSpec((1, D_MODEL), lambda i, j: (0, 0))],
        out_specs=pl.BlockSpec((D_MODEL, tm), lambda i, j: (0, i)),
        out_shape=jax.ShapeDtypeStruct((D_MODEL, n_tok), F32),
        scratch_shapes=[pltpu.VMEM((tm, D_MODEL), F32)],
        compiler_params=_params("arbitrary", "arbitrary"),
        name="mix_ln1",
    )(x_bf, x_f32, attn_o, conv_o, w_in_bf, w_in_bf, b_gate, b_gate, w_ao, w_co, w_o, ln_g, ln_b)


def _merge_exchange(n):
    pairs = []
    t = max(1, math.ceil(math.log2(n)))
    p = 1 << (t - 1)
    while p > 0:
        q, r, d = 1 << (t - 1), 0, p
        while d > 0:
            pairs += [(i, i + d) for i in range(n - d) if (i & p) == r]
            d, q, r = q - p, q >> 1, p
        p >>= 1
    return pairs


def _top_values(groups, k):
    v = list(groups)
    for i, j in _merge_exchange(len(v)):
        v[i], v[j] = jnp.maximum(v[i], v[j]), jnp.minimum(v[i], v[j])
    v.append(jnp.full(v[0].shape, -jnp.inf, F32))
    out = []
    for r in range(k):
        m = jnp.max(v[0], axis=0, keepdims=True)
        out.append(m)
        pop = v[0] == m
        for d in range(min(len(v) - 1, k - 1 - r)):
            v[d] = jnp.where(pop, v[d + 1], v[d])
    return out


def _row_groups(x):
    return [x[V7X_SUBLANES * g:V7X_SUBLANES * (g + 1)] for g in range(x.shape[0] // V7X_SUBLANES)]


def _candidate_groups(a, b):
    sub = lax.broadcasted_iota(jnp.int32, (V7X_SUBLANES, a[0].shape[1]), 0)
    pad = (jnp.full_like(a[0], -jnp.inf), jnp.zeros_like(b[0]))
    rows = [(a[r1], b[r2]) for r1, r2 in _CAND] + [pad] * (N_CAND_ROWS - len(_CAND))
    sums, prods = [], []
    for g in range(N_CAND_ROWS // V7X_SUBLANES):
        grp = rows[V7X_SUBLANES * g:V7X_SUBLANES * (g + 1)]
        ca = jnp.broadcast_to(grp[0][0], sub.shape)
        cb = jnp.broadcast_to(grp[0][1], sub.shape)
        for j in range(1, V7X_SUBLANES):
            if grp[j][0] is not grp[j - 1][0]:
                ca = jnp.where(sub >= j, grp[j][0], ca)
            if grp[j][1] is not grp[j - 1][1]:
                cb = jnp.where(sub >= j, grp[j][1], cb)
        sums.append(ca + cb)
        prods.append(jnp.exp(ca - a[0]) * jnp.exp(cb - b[0]))
    return sums, prods


def _count_above(rows, x):
    n = jnp.zeros(x.shape, F32)
    for r, row in enumerate(rows):
        n = jnp.where(row > x, float(r + 1), n)
    return n


def _route_kernel(hT_ref, wqT_ref, sk_ref, *refs):
    if len(refs) == 5:
        n1_ref, e1_ref, r2_ref, e2_ref, qT_ref = refs
    else:
        u_ref, v_ref, n1_ref, e1_ref, r2_ref, e2_ref, ub_ref, vT_ref, qT_ref = refs
        ub_ref[...] = u_ref[...].astype(BF16)
        vT_ref[...] = v_ref[...].T.astype(BF16)
    qT_ref[...] = _dot(wqT_ref[...], hT_ref[...].astype(BF16))

    def head(h, carry):
        base = pl.multiple_of(h * PEER_QDIM, PEER_QDIM)
        q1 = qT_ref[pl.ds(base, PEER_HALF), :].astype(BF16)
        q2 = qT_ref[pl.ds(base + PEER_HALF, PEER_HALF), :].astype(BF16)
        s1 = _dot(sk_ref[h, 0], q1)
        s2 = _dot(sk_ref[h, 1], q2)
        a = _top_values(_row_groups(s1), N_RANKS)
        b = _top_values(_row_groups(s2), N_RANKS)
        csum, cprod = _candidate_groups(a, b)
        c = _top_values(csum, N_RANKS)
        thr = 0.5 * (c[PEER_TOPK - 1] + c[PEER_TOPK])
        picked = [jnp.where(s >= c[PEER_TOPK - 1], p, 0.0) for s, p in zip(csum, cprod)]
        denom = jnp.sum(functools.reduce(jnp.add, picked), axis=0, keepdims=True)
        split = (PEER_N_KEYS // V7X_SUBLANES, V7X_SUBLANES, s1.shape[1])
        n1_ref[h] = _count_above(b[:PEER_TOPK], thr - s1).reshape(split)
        e1_ref[h] = (jnp.exp(s1 - a[0]) * (1.0 / denom)).reshape(split)
        r2_ref[h] = _count_above(b[:PEER_TOPK], s2).astype(BF16)
        e2_ref[h] = jnp.exp(s2 - b[0]).astype(BF16)
        return carry

    lax.fori_loop(0, PEER_HEADS, head, 0)


PEER_TABLE_ROWS_MAX = 512


def _peer_route(hT, wqT_bf, sub_keys_bf, u_tab, v_tab, tm):
    n_tok = hT.shape[1]
    steps = n_tok // tm
    groups = PEER_N_KEYS // V7X_SUBLANES
    out4 = jax.ShapeDtypeStruct((PEER_HEADS, groups, V7X_SUBLANES, n_tok), F32)
    out3 = jax.ShapeDtypeStruct((PEER_HEADS, PEER_N_KEYS, n_tok), BF16)
    spec4 = pl.BlockSpec((PEER_HEADS, groups, V7X_SUBLANES, tm), lambda i: (0, 0, 0, i))
    spec3 = pl.BlockSpec((PEER_HEADS, PEER_N_KEYS, tm), lambda i: (0, 0, i))
    in_specs = [pl.BlockSpec((D_MODEL, tm), lambda i: (0, i)),
                pl.BlockSpec((PEER_HEADS * PEER_QDIM, D_MODEL), lambda i: (0, 0), pipeline_mode=pl.Buffered(1)),
                pl.BlockSpec((PEER_HEADS, 2, PEER_N_KEYS, PEER_HALF), lambda i: (0, 0, 0, 0))]
    out_specs, out_shape, args = [spec4, spec4, spec3, spec3], [out4, out4, out3, out3], [hT, wqT_bf, sub_keys_bf]
    rows = PEER_N_EXPERTS // steps
    fuse_tables = rows * steps == PEER_N_EXPERTS and rows % V7X_LANES == 0 and rows <= PEER_TABLE_ROWS_MAX
    if fuse_tables:
        in_specs += [pl.BlockSpec((rows, D_MODEL), lambda i: (i, 0))] * 2
        out_specs += [pl.BlockSpec((rows, D_MODEL), lambda i: (i, 0)), pl.BlockSpec((D_MODEL, rows), lambda i: (0, i))]
        out_shape += [jax.ShapeDtypeStruct((PEER_N_EXPERTS, D_MODEL), BF16),
                      jax.ShapeDtypeStruct((D_MODEL, PEER_N_EXPERTS), BF16)]
        args += [u_tab, v_tab]
    outs = pl.pallas_call(
        _route_kernel,
        grid=(steps,),
        in_specs=in_specs,
        out_specs=out_specs,
        out_shape=out_shape,
        scratch_shapes=[pltpu.VMEM((PEER_HEADS * PEER_QDIM, tm), F32)],
        compiler_params=_params("arbitrary"),
        name="peer_route",
    )(*args)
    if not fuse_tables:
        outs = list(outs) + [u_tab.astype(BF16), v_tab.T.astype(BF16)]
    return outs


def _gelu(x):
    return 0.5 * x * (1.0 + lax.erf(x * (2.0 ** -0.5)))


def _rows_bf16(row, n_rows):
    packed_rows = 2 * V7X_SUBLANES
    tile = jnp.broadcast_to(row, (packed_rows, row.shape[1])).astype(BF16)
    return jnp.concatenate([tile] * (n_rows // packed_rows), axis=0)


def _peer_kernel(hT_ref, u_ref, vT_ref, n1_ref, e1_ref, r2_ref, e2_ref, g_ref, b_ref, o_ref,
                 hb_ref, acc_ref):
    j = pl.program_id(1)

    @pl.when(j == 0)
    def _():
        hb_ref[...] = hT_ref[...].astype(BF16)
        acc_ref[...] = jnp.zeros_like(acc_ref)

    act = _gelu(_dot(u_ref[...], hb_ref[...]).astype(BF16))
    slabs = []
    for i1 in range(act.shape[0] // PEER_N_KEYS):
        grp, sub = divmod(i1, V7X_SUBLANES)
        w = None
        for h in range(PEER_HEADS):
            sel = r2_ref[h] < _rows_bf16(n1_ref[h, grp, sub:sub + 1, :], PEER_N_KEYS)
            wh = jnp.where(sel, _rows_bf16(e1_ref[h, grp, sub:sub + 1, :], PEER_N_KEYS), 0.0) * e2_ref[h]
            w = wh if w is None else w + wh
        slabs.append(act[i1 * PEER_N_KEYS:(i1 + 1) * PEER_N_KEYS] * w)
    acc_ref[...] += _dot(vT_ref[...], jnp.concatenate(slabs, axis=0))

    @pl.when(j == pl.num_programs(1) - 1)
    def _():
        r = (ALPHA * hT_ref[...] + acc_ref[...]).T
        o_ref[...] = _layer_norm(r, g_ref[...], b_ref[...])


def _peer_experts(hT, u_bf, vT_bf, n1, e1, r2, e2, ln_g, ln_b, tm, te):
    n_tok = hT.shape[1]
    groups = te // (PEER_N_KEYS * V7X_SUBLANES)
    assert groups * PEER_N_KEYS * V7X_SUBLANES == te
    tok = lambda i, j: (0, 0, i)
    blk = lambda i, j: (0, j, 0, i)
    return pl.pallas_call(
        _peer_kernel,
        grid=(n_tok // tm, PEER_N_EXPERTS // te),
        in_specs=[pl.BlockSpec((D_MODEL, tm), lambda i, j: (0, i)),
                  pl.BlockSpec((te, D_MODEL), lambda i, j: (j, 0)),
                  pl.BlockSpec((D_MODEL, te), lambda i, j: (0, j)),
                  pl.BlockSpec((PEER_HEADS, groups, V7X_SUBLANES, tm), blk),
                  pl.BlockSpec((PEER_HEADS, groups, V7X_SUBLANES, tm), blk),
                  pl.BlockSpec((PEER_HEADS, PEER_N_KEYS, tm), tok),
                  pl.BlockSpec((PEER_HEADS, PEER_N_KEYS, tm), tok),
                  pl.BlockSpec((1, D_MODEL), lambda i, j: (0, 0)),
                  pl.BlockSpec((1, D_MODEL), lambda i, j: (0, 0))],
        out_specs=pl.BlockSpec((tm, D_MODEL), lambda i, j: (i, 0)),
        out_shape=jax.ShapeDtypeStruct((n_tok, D_MODEL), F32),
        scratch_shapes=[pltpu.VMEM((D_MODEL, tm), BF16), pltpu.VMEM((D_MODEL, tm), F32)],
        compiler_params=_params("arbitrary", "arbitrary"),
        name="peer_experts",
    )(hT, u_bf, vT_bf, n1, e1, r2, e2, ln_g, ln_b)


def _layer(h, batch, seq, w_in, b_gate, sinks, conv_w, w_ao, w_co, w_o, ln1_g, ln1_b,
           w_q, sub_keys, u_tab, v_tab, ln2_g, ln2_b):
    n_tok = h.shape[0]
    t = _tiles(n_tok, seq)
    row = lambda v: v.reshape(1, -1)
    h_bf = h.astype(BF16)
    qkv = _qkv_proj(h_bf, w_in, t["proj"], t["col"])
    attn_o, w_in_bf, w_ao_bf, w_co_bf, w_o_bf = _attention(qkv, sinks, batch, seq, [w_in, w_ao, w_co, w_o])
    conv_o = _conv_branch(h_bf, w_in_bf, conv_w, seq, t["proj"], t["col"])
    hT = _mix_ln1(h_bf, h, attn_o, conv_o, w_in_bf, row(b_gate), w_ao_bf, w_co_bf, w_o_bf,
                  row(ln1_g), row(ln1_b), t["mix"], t["mix_col"])
    n1, e1, r2, e2, u_bf, vT_bf = _peer_route(hT, w_q.T.astype(BF16), sub_keys.astype(BF16), u_tab, v_tab,
                                              t["route"])
    return _peer_experts(hT, u_bf, vT_bf, n1, e1, r2, e2, row(ln2_g), row(ln2_b), t["peer"], t["experts"])


def kernel(x, w_in, b_gate, sinks, conv_w, w_attn_out, w_conv_out, w_o, ln1_g, ln1_b, peer_w_q,
           peer_sub_keys, peer_u, peer_v, ln2_g, ln2_b):
    batch, seq, d_model = x.shape
    assert d_model == D_MODEL and seq % WINDOW == 0
    h = x.reshape(batch * seq, d_model)
    for l in range(w_in.shape[0]):
        h = _layer(h, batch, seq, w_in[l], b_gate[l], sinks[l], conv_w[l], w_attn_out[l],
                   w_conv_out[l], w_o[l], ln1_g[l], ln1_b[l], peer_w_q[l], peer_sub_keys[l],
                   peer_u[l], peer_v[l], ln2_g[l], ln2_b[l])
    return h.reshape(batch, seq, d_model)
```

```python
import functools
import math

import jax
import jax.numpy as jnp
from jax import lax
from jax.experimental import pallas as pl
from jax.experimental.pallas import tpu as pltpu

F32 = jnp.float32
BF16 = jnp.bfloat16

D_MODEL = 2048
N_Q_HEADS = 32
N_KV_HEADS = 4
HEAD_DIM = 64
Q_PER_KV = N_Q_HEADS // N_KV_HEADS
WINDOW = 128
ATTN_WIDTH = N_Q_HEADS * HEAD_DIM
KV_WIDTH = N_KV_HEADS * HEAD_DIM
QKV_WIDTH = ATTN_WIDTH + 2 * KV_WIDTH
CONV_WIDTH = D_MODEL
CONV_K = 3
K_OFF = ATTN_WIDTH
V_OFF = K_OFF + KV_WIDTH
CB_OFF = V_OFF + KV_WIDTH
CC_OFF = CB_OFF + CONV_WIDTH
CH_OFF = CC_OFF + CONV_WIDTH
G_OFF = CH_OFF + CONV_WIDTH

PEER_HEADS = 8
PEER_N_KEYS = 128
PEER_N_EXPERTS = PEER_N_KEYS * PEER_N_KEYS
PEER_TOPK = 16
PEER_HALF = 128
PEER_QDIM = 2 * PEER_HALF

LN_EPS = 1e-5
DEPTH = 1
ALPHA = (2.0 * DEPTH) ** 0.25

V7X_LANES = 128
V7X_SUBLANES = 8
V7X_VMEM_LIMIT_BYTES = 56 * 1024 * 1024

N_RANKS = PEER_TOPK + 1
_CAND = [(r1, r2) for r1 in range(N_RANKS) for r2 in range(N_RANKS) if (r1 + 1) * (r2 + 1) <= N_RANKS]
N_CAND_ROWS = -(-len(_CAND) // V7X_SUBLANES) * V7X_SUBLANES


def _tiles(n_tokens, seq):
    return dict(
        proj=min(1024, seq),
        mix=min(512, n_tokens),
        route=min(256, n_tokens),
        peer=min(512, n_tokens),
        col=512,
        mix_col=256,
        experts=1024,
    )


def _params(*sem):
    return pltpu.CompilerParams(dimension_semantics=sem, vmem_limit_bytes=V7X_VMEM_LIMIT_BYTES)


def _dot(a, b):
    return jnp.dot(a, b, preferred_element_type=F32)


def _qkv_kernel(x_ref, w_ref, o_ref, wb_ref):
    @pl.when(pl.program_id(1) == 0)
    def _():
        wb_ref[...] = w_ref[...].astype(BF16)

    o_ref[...] = _dot(x_ref[...], wb_ref[...]).astype(o_ref.dtype)


def _qkv_proj(x_bf, w_in, tm, tn):
    n_tok = x_bf.shape[0]
    return pl.pallas_call(
        _qkv_kernel,
        grid=(QKV_WIDTH // tn, n_tok // tm),
        in_specs=[pl.BlockSpec((tm, D_MODEL), lambda j, i: (i, 0)),
                  pl.BlockSpec((D_MODEL, tn), lambda j, i: (0, j))],
        out_specs=pl.BlockSpec((tm, tn), lambda j, i: (i, j)),
        out_shape=jax.ShapeDtypeStruct((n_tok, QKV_WIDTH), BF16),
        scratch_shapes=[pltpu.VMEM((D_MODEL, tn), BF16)],
        compiler_params=_params("arbitrary", "arbitrary"),
        name="qkv_proj",
    )(x_bf, w_in)


def _conv_kernel(x_ref, wcb_ref, wcc_ref, wch_ref, cw_ref, o_ref, carry_ref, *, tiles_per_seq):
    i = pl.program_id(1)
    x = x_ref[...]
    cb = _dot(x, wcb_ref[...])
    u = _dot(x, wcc_ref[...]) * _dot(x, wch_ref[...])
    tm = u.shape[0]

    @pl.when(i % tiles_per_seq == 0)
    def _():
        carry_ref[...] = jnp.zeros_like(carry_ref)

    prev = carry_ref[...]
    row = lax.broadcasted_iota(jnp.int32, u.shape, 0)
    u1 = jnp.where(row == 0, prev[7:8], pltpu.roll(u, 1, 0))
    u2 = pltpu.roll(u, 2, 0)
    u2 = jnp.where(row == 0, prev[6:7], jnp.where(row == 1, prev[7:8], u2))
    cw = cw_ref[...]
    y = cb * (cw[0:1] * u2 + cw[1:2] * u1 + cw[2:3] * u)
    o_ref[...] = y.astype(o_ref.dtype)
    carry_ref[...] = u[tm - V7X_SUBLANES:tm]


def _conv_branch(x_bf, w_in_bf, conv_w, seq, tm, nc):
    n_tok = x_bf.shape[0]
    wspec = lambda off: pl.BlockSpec((D_MODEL, nc), lambda j, i, o=off // nc: (0, o + j))
    return pl.pallas_call(
        functools.partial(_conv_kernel, tiles_per_seq=seq // tm),
        grid=(CONV_WIDTH // nc, n_tok // tm),
        in_specs=[pl.BlockSpec((tm, D_MODEL), lambda j, i: (i, 0)),
                  wspec(CB_OFF), wspec(CC_OFF), wspec(CH_OFF),
                  pl.BlockSpec((CONV_K, nc), lambda j, i: (0, j))],
        out_specs=pl.BlockSpec((tm, nc), lambda j, i: (i, j)),
        out_shape=jax.ShapeDtypeStruct((n_tok, CONV_WIDTH), BF16),
        scratch_shapes=[pltpu.VMEM((V7X_SUBLANES, nc), F32)],
        compiler_params=_params("arbitrary", "arbitrary"),
        name="conv_branch",
    )(x_bf, w_in_bf, w_in_bf, w_in_bf, conv_w)


def _attn_kernel(sinks_ref, q_ref, kc_ref, kp_ref, vc_ref, vp_ref, *refs):
    n_side = (len(refs) - 1) // 2
    o_ref = refs[n_side]
    for src, dst in zip(refs[:n_side], refs[n_side + 1:]):
        dst[...] = src[...].astype(BF16)
    n = pl.program_id(1)
    cols = Q_PER_KV * WINDOW
    kj = lax.broadcasted_iota(jnp.int32, (2 * WINDOW, cols), 0)
    qi = lax.broadcasted_iota(jnp.int32, (2 * WINDOW, cols), 1) & (WINDOW - 1)
    rel = WINDOW + qi - kj
    mask = (rel >= 0) & (rel < WINDOW) & ((kj >= WINDOW) | (n > 0))
    scale = HEAD_DIM ** -0.5
    for j in range(N_KV_HEADS):
        heads = [j * Q_PER_KV + g for g in range(Q_PER_KV)]
        qg = jnp.concatenate([q_ref[:, h * HEAD_DIM:(h + 1) * HEAD_DIM] for h in heads], axis=0)
        ksl = slice(j * HEAD_DIM, (j + 1) * HEAD_DIM)
        kw = jnp.concatenate([kp_ref[:, ksl], kc_ref[:, ksl]], axis=0)
        vw = jnp.concatenate([vp_ref[:, ksl], vc_ref[:, ksl]], axis=0)
        s = lax.dot_general(kw, qg * scale, (((1,), (1,)), ((), ())), preferred_element_type=F32)
        s = jnp.where(mask, s, -jnp.inf)
        sink = jnp.concatenate([jnp.full((1, WINDOW), sinks_ref[h], F32) for h in heads], axis=1)
        m = jnp.maximum(jnp.max(s, axis=0, keepdims=True), sink)
        p = jnp.exp(s - m)
        denom = jnp.sum(p, axis=0, keepdims=True) + jnp.exp(sink - m)
        probs = (p / denom).astype(BF16)
        o = lax.dot_general(probs, vw, (((0,), (0,)), ((), ())), preferred_element_type=F32)
        for g, h in enumerate(heads):
            o_ref[:, h * HEAD_DIM:(h + 1) * HEAD_DIM] = o[g * WINDOW:(g + 1) * WINDOW].astype(o_ref.dtype)


ATTN_SIDE_ROWS_MAX = 64


def _attention(qkv, sinks, batch, seq, weights):
    n_tok = qkv.shape[0]
    nb = seq // WINDOW
    cur = lambda b, n: b * nb + n
    prev = lambda b, n: b * nb + jnp.maximum(n - 1, 0)
    kcol, vcol = K_OFF // KV_WIDTH, V_OFF // KV_WIDTH
    rows = D_MODEL // (batch * nb)
    fuse = (rows * batch * nb == D_MODEL and rows % (2 * V7X_SUBLANES) == 0 and rows <= ATTN_SIDE_ROWS_MAX)
    side = weights if fuse else []
    wspec = lambda w: pl.BlockSpec((rows, w.shape[1]), lambda b, n: (cur(b, n), 0))
    outs = pl.pallas_call(
        _attn_kernel,
        grid=(batch, nb),
        in_specs=[pl.BlockSpec(memory_space=pltpu.SMEM),
                  pl.BlockSpec((WINDOW, ATTN_WIDTH), lambda b, n: (cur(b, n), 0)),
                  pl.BlockSpec((WINDOW, KV_WIDTH), lambda b, n: (cur(b, n), kcol)),
                  pl.BlockSpec((WINDOW, KV_WIDTH), lambda b, n: (prev(b, n), kcol)),
                  pl.BlockSpec((WINDOW, KV_WIDTH), lambda b, n: (cur(b, n), vcol)),
                  pl.BlockSpec((WINDOW, KV_WIDTH), lambda b, n: (prev(b, n), vcol))]
        + [wspec(w) for w in side],
        out_specs=[pl.BlockSpec((WINDOW, ATTN_WIDTH), lambda b, n: (cur(b, n), 0))] + [wspec(w) for w in side],
        out_shape=[jax.ShapeDtypeStruct((n_tok, ATTN_WIDTH), BF16)]
        + [jax.ShapeDtypeStruct(w.shape, BF16) for w in side],
        compiler_params=_params("arbitrary", "arbitrary"),
        name="swa_attention",
    )(sinks, qkv, qkv, qkv, qkv, qkv, *side)
    if not fuse:
        outs = list(outs) + [w.astype(BF16) for w in weights]
    return outs


def _layer_norm(r, g, b):
    mu = jnp.mean(r, axis=-1, keepdims=True)
    d = r - mu
    var = jnp.mean(d * d, axis=-1, keepdims=True)
    return d * lax.rsqrt(var + LN_EPS) * g + b


def _sigmoid(z):
    return 1.0 / (1.0 + jnp.exp(-z))


def _mix_kernel(xb_ref, xf_ref, a_ref, c_ref, wga_ref, wgc_ref, bga_ref, bgc_ref, wao_ref, wco_ref,
                wo_ref, g_ref, b_ref, hT_ref, acc_ref):
    j = pl.program_id(1)

    @pl.when(j == 0)
    def _():
        acc_ref[...] = jnp.zeros_like(acc_ref)

    xb = xb_ref[...]
    ga = _sigmoid(_dot(xb, wga_ref[...]) + bga_ref[...])
    gc = _sigmoid(_dot(xb, wgc_ref[...]) + bgc_ref[...])
    merged = ga * _dot(a_ref[...], wao_ref[...]) + gc * _dot(c_ref[...], wco_ref[...])
    acc_ref[...] += _dot(merged.astype(BF16), wo_ref[...])

    @pl.when(j == pl.num_programs(1) - 1)
    def _():
        h1 = _layer_norm(ALPHA * xf_ref[...] + acc_ref[...], g_ref[...], b_ref[...])
        hT_ref[...] = h1.T


def _mix_ln1(x_bf, x_f32, attn_o, conv_o, w_in_bf, b_gate, w_ao, w_co, w_o, ln_g, ln_b, tm, nc):
    n_tok = x_bf.shape[0]
    row = lambda i, j: (i, 0)
    gcol = G_OFF // nc
    return pl.pallas_call(
        _mix_kernel,
        grid=(n_tok // tm, D_MODEL // nc),
        in_specs=[pl.BlockSpec((tm, D_MODEL), row), pl.BlockSpec((tm, D_MODEL), row),
                  pl.BlockSpec((tm, ATTN_WIDTH), row), pl.BlockSpec((tm, CONV_WIDTH), row),
                  pl.BlockSpec((D_MODEL, nc), lambda i, j: (0, gcol + j)),
                  pl.BlockSpec((D_MODEL, nc), lambda i, j: (0, gcol + D_MODEL // nc + j)),
                  pl.BlockSpec((1, nc), lambda i, j: (0, j)),
                  pl.BlockSpec((1, nc), lambda i, j: (0, D_MODEL // nc + j)),
                  pl.BlockSpec((ATTN_WIDTH, nc), lambda i, j: (0, j)),
                  pl.BlockSpec((CONV_WIDTH, nc), lambda i, j: (0, j)),
                  pl.BlockSpec((nc, D_MODEL), lambda i, j: (j, 0)),
                  pl.BlockSpec((1, D_MODEL), lambda i, j: (0, 0)),
                  pl.BlockSpec((1, D_MODEL), lambda i, j: (0, 0))],
        out_specs=pl.BlockSpec((D_MODEL, tm), lambda i, j: (0, i)),
        out_shape=jax.ShapeDtypeStruct((D_MODEL, n_tok), F32),
        scratch_shapes=[pltpu.VMEM((tm, D_MODEL), F32)],
        compiler_params=_params("arbitrary", "arbitrary"),
        name="mix_ln1",
    )(x_bf, x_f32, attn_o, conv_o, w_in_bf, w_in_bf, b_gate, b_gate, w_ao, w_co, w_o, ln_g, ln_b)


def _merge_exchange(n):
    pairs = []
    t = max(1, math.ceil(math.log2(n)))
    p = 1 << (t - 1)
    while p > 0:
        q, r, d = 1 << (t - 1), 0, p
        while d > 0:
            pairs += [(i, i + d) for i in range(n - d) if (i & p) == r]
            d, q, r = q - p, q >> 1, p
        p >>= 1
    return pairs


def _top_values(groups, k):
    v = list(groups)
    for i, j in _merge_exchange(len(v)):
        v[i], v[j] = jnp.maximum(v[i], v[j]), jnp.minimum(v[i], v[j])
    v.append(jnp.full(v[0].shape, -jnp.inf, F32))
    out = []
    for r in range(k):
        m = jnp.max(v[0], axis=0, keepdims=True)
        out.append(m)
        pop = v[0] == m
        for d in range(min(len(v) - 1, k - 1 - r)):
            v[d] = jnp.where(pop, v[d + 1], v[d])
    return out


def _row_groups(x):
    return [x[V7X_SUBLANES * g:V7X_SUBLANES * (g + 1)] for g in range(x.shape[0] // V7X_SUBLANES)]


def _candidate_groups(a, b):
    sub = lax.broadcasted_iota(jnp.int32, (V7X_SUBLANES, a[0].shape[1]), 0)
    pad = (jnp.full_like(a[0], -jnp.inf), jnp.zeros_like(b[0]))
    rows = [(a[r1], b[r2]) for r1, r2 in _CAND] + [pad] * (N_CAND_ROWS - len(_CAND))
    sums, prods = [], []
    for g in range(N_CAND_ROWS // V7X_SUBLANES):
        grp = rows[V7X_SUBLANES * g:V7X_SUBLANES * (g + 1)]
        ca = jnp.broadcast_to(grp[0][0], sub.shape)
        cb = jnp.broadcast_to(grp[0][1], sub.shape)
        for j in range(1, V7X_SUBLANES):
            if grp[j][0] is not grp[j - 1][0]:
                ca = jnp.where(sub >= j, grp[j][0], ca)
            if grp[j][1] is not grp[j - 1][1]:
                cb = jnp.where(sub >= j, grp[j][1], cb)
        sums.append(ca + cb)
        prods.append(jnp.exp(ca - a[0]) * jnp.exp(cb - b[0]))
    return sums, prods


def _count_above(rows, x):
    n = jnp.zeros(x.shape, F32)
    for r, row in enumerate(rows):
        n = jnp.where(row > x, float(r + 1), n)
    return n


def _route_kernel(hT_ref, wqT_ref, sk_ref, *refs):
    if len(refs) == 4:
        n1_ref, e1_ref, r2_ref, e2_ref = refs
    else:
        u_ref, v_ref, n1_ref, e1_ref, r2_ref, e2_ref, ub_ref, vT_ref = refs
        ub_ref[...] = u_ref[...].astype(BF16)
        vT_ref[...] = v_ref[...].T.astype(BF16)
    qT = _dot(wqT_ref[...], hT_ref[...].astype(BF16)).astype(BF16)

    for h in range(PEER_HEADS):
        q1 = qT[h * PEER_QDIM:h * PEER_QDIM + PEER_HALF]
        q2 = qT[h * PEER_QDIM + PEER_HALF:(h + 1) * PEER_QDIM]
        s1 = _dot(sk_ref[h, 0], q1)
        s2 = _dot(sk_ref[h, 1], q2)
        a = _top_values(_row_groups(s1), N_RANKS)
        b = _top_values(_row_groups(s2), N_RANKS)
        csum, cprod = _candidate_groups(a, b)
        c = _top_values(csum, N_RANKS)
        thr = 0.5 * (c[PEER_TOPK - 1] + c[PEER_TOPK])
        picked = [jnp.where(s >= c[PEER_TOPK - 1], p, 0.0) for s, p in zip(csum, cprod)]
        denom = jnp.sum(functools.reduce(jnp.add, picked), axis=0, keepdims=True)
        split = (PEER_N_KEYS // V7X_SUBLANES, V7X_SUBLANES, s1.shape[1])
        n1_ref[h] = _count_above(b[:PEER_TOPK], thr - s1).reshape(split)
        e1_ref[h] = (jnp.exp(s1 - a[0]) * (1.0 / denom)).reshape(split)
        r2_ref[h] = _count_above(b[:PEER_TOPK], s2).astype(BF16)
        e2_ref[h] = jnp.exp(s2 - b[0]).astype(BF16)


PEER_TABLE_ROWS_MAX = 512


def _peer_route(hT, wqT_bf, sub_keys_bf, u_tab, v_tab, tm):
    n_tok = hT.shape[1]
    steps = n_tok // tm
    groups = PEER_N_KEYS // V7X_SUBLANES
    out4 = jax.ShapeDtypeStruct((PEER_HEADS, groups, V7X_SUBLANES, n_tok), F32)
    out3 = jax.ShapeDtypeStruct((PEER_HEADS, PEER_N_KEYS, n_tok), BF16)
    spec4 = pl.BlockSpec((PEER_HEADS, groups, V7X_SUBLANES, tm), lambda i: (0, 0, 0, i))
    spec3 = pl.BlockSpec((PEER_HEADS, PEER_N_KEYS, tm), lambda i: (0, 0, i))
    in_specs = [pl.BlockSpec((D_MODEL, tm), lambda i: (0, i)),
                pl.BlockSpec((PEER_HEADS * PEER_QDIM, D_MODEL), lambda i: (0, 0), pipeline_mode=pl.Buffered(1)),
                pl.BlockSpec((PEER_HEADS, 2, PEER_N_KEYS, PEER_HALF), lambda i: (0, 0, 0, 0))]
    out_specs, out_shape, args = [spec4, spec4, spec3, spec3], [out4, out4, out3, out3], [hT, wqT_bf, sub_keys_bf]
    rows = PEER_N_EXPERTS // steps
    fuse_tables = rows * steps == PEER_N_EXPERTS and rows % V7X_LANES == 0 and rows <= PEER_TABLE_ROWS_MAX
    if fuse_tables:
        in_specs += [pl.BlockSpec((rows, D_MODEL), lambda i: (i, 0))] * 2
        out_specs += [pl.BlockSpec((rows, D_MODEL), lambda i: (i, 0)), pl.BlockSpec((D_MODEL, rows), lambda i: (0, i))]
        out_shape += [jax.ShapeDtypeStruct((PEER_N_EXPERTS, D_MODEL), BF16),
                      jax.ShapeDtypeStruct((D_MODEL, PEER_N_EXPERTS), BF16)]
        args += [u_tab, v_tab]
    outs = pl.pallas_call(
        _route_kernel,
        grid=(steps,),
        in_specs=in_specs,
        out_specs=out_specs,
        out_shape=out_shape,
        compiler_params=_params("arbitrary"),
        name="peer_route",
    )(*args)
    if not fuse_tables:
        outs = list(outs) + [u_tab.astype(BF16), v_tab.T.astype(BF16)]
    return outs


def _gelu(x):
    return 0.5 * x * (1.0 + lax.erf(x * (2.0 ** -0.5)))


def _rows_bf16(row, n_rows):
    packed_rows = 2 * V7X_SUBLANES
    tile = jnp.broadcast_to(row, (packed_rows, row.shape[1])).astype(BF16)
    return jnp.concatenate([tile] * (n_rows // packed_rows), axis=0)


def _peer_kernel(hT_ref, u_ref, vT_ref, n1_ref, e1_ref, r2_ref, e2_ref, g_ref, b_ref, o_ref,
                 hb_ref, acc_ref):
    j = pl.program_id(1)

    @pl.when(j == 0)
    def _():
        hb_ref[...] = hT_ref[...].astype(BF16)
        acc_ref[...] = jnp.zeros_like(acc_ref)

    act = _gelu(_dot(u_ref[...], hb_ref[...]).astype(BF16))
    slabs = []
    for i1 in range(act.shape[0] // PEER_N_KEYS):
        grp, sub = divmod(i1, V7X_SUBLANES)
        w = None
        for h in range(PEER_HEADS):
            sel = r2_ref[h] < _rows_bf16(n1_ref[h, grp, sub:sub + 1, :], PEER_N_KEYS)
            wh = jnp.where(sel, _rows_bf16(e1_ref[h, grp, sub:sub + 1, :], PEER_N_KEYS), 0.0) * e2_ref[h]
            w = wh if w is None else w + wh
        slabs.append(act[i1 * PEER_N_KEYS:(i1 + 1) * PEER_N_KEYS] * w)
    acc_ref[...] += _dot(vT_ref[...], jnp.concatenate(slabs, axis=0))

    @pl.when(j == pl.num_programs(1) - 1)
    def _():
        r = (ALPHA * hT_ref[...] + acc_ref[...]).T
        o_ref[...] = _layer_norm(r, g_ref[...], b_ref[...])


def _peer_experts(hT, u_bf, vT_bf, n1, e1, r2, e2, ln_g, ln_b, tm, te):
    n_tok = hT.shape[1]
    groups = te // (PEER_N_KEYS * V7X_SUBLANES)
    assert groups * PEER_N_KEYS * V7X_SUBLANES == te
    tok = lambda i, j: (0, 0, i)
    blk = lambda i, j: (0, j, 0, i)
    return pl.pallas_call(
        _peer_kernel,
        grid=(n_tok // tm, PEER_N_EXPERTS // te),
        in_specs=[pl.BlockSpec((D_MODEL, tm), lambda i, j: (0, i)),
                  pl.BlockSpec((te, D_MODEL), lambda i, j: (j, 0)),
                  pl.BlockSpec((D_MODEL, te), lambda i, j: (0, j)),
                  pl.BlockSpec((PEER_HEADS, groups, V7X_SUBLANES, tm), blk),
                  pl.BlockSpec((PEER_HEADS, groups, V7X_SUBLANES, tm), blk),
                  pl.BlockSpec((PEER_HEADS, PEER_N_KEYS, tm), tok),
                  pl.BlockSpec((PEER_HEADS, PEER_N_KEYS, tm), tok),
                  pl.BlockSpec((1, D_MODEL), lambda i, j: (0, 0)),
                  pl.BlockSpec((1, D_MODEL), lambda i, j: (0, 0))],
        out_specs=pl.BlockSpec((tm, D_MODEL), lambda i, j: (i, 0)),
        out_shape=jax.ShapeDtypeStruct((n_tok, D_MODEL), F32),
        scratch_shapes=[pltpu.VMEM((D_MODEL, tm), BF16), pltpu.VMEM((D_MODEL, tm), F32)],
        compiler_params=_params("arbitrary", "arbitrary"),
        name="peer_experts",
    )(hT, u_bf, vT_bf, n1, e1, r2, e2, ln_g, ln_b)


def _layer(h, batch, seq, w_in, b_gate, sinks, conv_w, w_ao, w_co, w_o, ln1_g, ln1_b,
           w_q, sub_keys, u_tab, v_tab, ln2_g, ln2_b):
    n_tok = h.shape[0]
    t = _tiles(n_tok, seq)
    row = lambda v: v.reshape(1, -1)
    h_bf = h.astype(BF16)
    qkv = _qkv_proj(h_bf, w_in, t["proj"], t["col"])
    attn_o, w_in_bf, w_ao_bf, w_co_bf, w_o_bf = _attention(qkv, sinks, batch, seq, [w_in, w_ao, w_co, w_o])
    conv_o = _conv_branch(h_bf, w_in_bf, conv_w, seq, t["proj"], t["col"])
    hT = _mix_ln1(h_bf, h, attn_o, conv_o, w_in_bf, row(b_gate), w_ao_bf, w_co_bf, w_o_bf,
                  row(ln1_g), row(ln1_b), t["mix"], t["mix_col"])
    n1, e1, r2, e2, u_bf, vT_bf = _peer_route(hT, w_q.T.astype(BF16), sub_keys.astype(BF16), u_tab, v_tab,
                                              t["route"])
    return _peer_experts(hT, u_bf, vT_bf, n1, e1, r2, e2, row(ln2_g), row(ln2_b), t["peer"], t["experts"])


def kernel(x, w_in, b_gate, sinks, conv_w, w_attn_out, w_conv_out, w_o, ln1_g, ln1_b, peer_w_q,
           peer_sub_keys, peer_u, peer_v, ln2_g, ln2_b):
    batch, seq, d_model = x.shape
    assert d_model == D_MODEL and seq % WINDOW == 0
    h = x.reshape(batch * seq, d_model)
    for l in range(w_in.shape[0]):
        h = _layer(h, batch, seq, w_in[l], b_gate[l], sinks[l], conv_w[l], w_attn_out[l],
                   w_conv_out[l], w_o[l], ln1_g[l], ln1_b[l], peer_w_q[l], peer_sub_keys[l],
                   peer_u[l], peer_v[l], ln2_g[l], ln2_b[l])
    return h.reshape(batch, seq, d_model)
```

```python
import functools
import math

import jax
import jax.numpy as jnp
from jax import lax
from jax.experimental import pallas as pl
from jax.experimental.pallas import tpu as pltpu

F32 = jnp.float32
BF16 = jnp.bfloat16

D_MODEL = 2048
N_Q_HEADS = 32
N_KV_HEADS = 4
HEAD_DIM = 64
Q_PER_KV = N_Q_HEADS // N_KV_HEADS
WINDOW = 128
ATTN_WIDTH = N_Q_HEADS * HEAD_DIM
KV_WIDTH = N_KV_HEADS * HEAD_DIM
QKV_WIDTH = ATTN_WIDTH + 2 * KV_WIDTH
CONV_WIDTH = D_MODEL
CONV_K = 3
K_OFF = ATTN_WIDTH
V_OFF = K_OFF + KV_WIDTH
CB_OFF = V_OFF + KV_WIDTH
CC_OFF = CB_OFF + CONV_WIDTH
CH_OFF = CC_OFF + CONV_WIDTH
G_OFF = CH_OFF + CONV_WIDTH

PEER_HEADS = 8
PEER_N_KEYS = 128
PEER_N_EXPERTS = PEER_N_KEYS * PEER_N_KEYS
PEER_TOPK = 16
PEER_HALF = 128
PEER_QDIM = 2 * PEER_HALF

LN_EPS = 1e-5
DEPTH = 1
ALPHA = (2.0 * DEPTH) ** 0.25

V7X_LANES = 128
V7X_SUBLANES = 8
V7X_VMEM_LIMIT_BYTES = 56 * 1024 * 1024

N_RANKS = PEER_TOPK + 1
_CAND = [(r1, r2) for r1 in range(N_RANKS) for r2 in range(N_RANKS) if (r1 + 1) * (r2 + 1) <= N_RANKS]
N_CAND_ROWS = -(-len(_CAND) // V7X_SUBLANES) * V7X_SUBLANES


def _tiles(n_tokens, seq):
    return dict(
        proj=min(1024, seq),
        mix=min(512, n_tokens),
        route=min(256, n_tokens),
        peer=min(512, n_tokens),
        col=512,
        mix_col=256,
        experts=1024,
    )


def _params(*sem):
    return pltpu.CompilerParams(dimension_semantics=sem, vmem_limit_bytes=V7X_VMEM_LIMIT_BYTES)


def _dot(a, b):
    return jnp.dot(a, b, preferred_element_type=F32)


def _qkv_kernel(x_ref, w_ref, o_ref, wb_ref):
    @pl.when(pl.program_id(1) == 0)
    def _():
        wb_ref[...] = w_ref[...].astype(BF16)

    o_ref[...] = _dot(x_ref[...], wb_ref[...]).astype(o_ref.dtype)


def _qkv_proj(x_bf, w_in, tm, tn):
    n_tok = x_bf.shape[0]
    return pl.pallas_call(
        _qkv_kernel,
        grid=(QKV_WIDTH // tn, n_tok // tm),
        in_specs=[pl.BlockSpec((tm, D_MODEL), lambda j, i: (i, 0)),
                  pl.BlockSpec((D_MODEL, tn), lambda j, i: (0, j))],
        out_specs=pl.BlockSpec((tm, tn), lambda j, i: (i, j)),
        out_shape=jax.ShapeDtypeStruct((n_tok, QKV_WIDTH), BF16),
        scratch_shapes=[pltpu.VMEM((D_MODEL, tn), BF16)],
        compiler_params=_params("arbitrary", "arbitrary"),
        name="qkv_proj",
    )(x_bf, w_in)


def _conv_kernel(x_ref, wcb_ref, wcc_ref, wch_ref, cw_ref, o_ref, carry_ref, *, tiles_per_seq):
    i = pl.program_id(1)
    x = x_ref[...]
    cb = _dot(x, wcb_ref[...])
    u = _dot(x, wcc_ref[...]) * _dot(x, wch_ref[...])
    tm = u.shape[0]

    @pl.when(i % tiles_per_seq == 0)
    def _():
        carry_ref[...] = jnp.zeros_like(carry_ref)

    prev = carry_ref[...]
    row = lax.broadcasted_iota(jnp.int32, u.shape, 0)
    u1 = jnp.where(row == 0, prev[7:8], pltpu.roll(u, 1, 0))
    u2 = pltpu.roll(u, 2, 0)
    u2 = jnp.where(row == 0, prev[6:7], jnp.where(row == 1, prev[7:8], u2))
    cw = cw_ref[...]
    y = cb * (cw[0:1] * u2 + cw[1:2] * u1 + cw[2:3] * u)
    o_ref[...] = y.astype(o_ref.dtype)
    carry_ref[...] = u[tm - V7X_SUBLANES:tm]


def _conv_branch(x_bf, w_in_bf, conv_w, seq, tm, nc):
    n_tok = x_bf.shape[0]
    wspec = lambda off: pl.BlockSpec((D_MODEL, nc), lambda j, i, o=off // nc: (0, o + j))
    return pl.pallas_call(
        functools.partial(_conv_kernel, tiles_per_seq=seq // tm),
        grid=(CONV_WIDTH // nc, n_tok // tm),
        in_specs=[pl.BlockSpec((tm, D_MODEL), lambda j, i: (i, 0)),
                  wspec(CB_OFF), wspec(CC_OFF), wspec(CH_OFF),
                  pl.BlockSpec((CONV_K, nc), lambda j, i: (0, j))],
        out_specs=pl.BlockSpec((tm, nc), lambda j, i: (i, j)),
        out_shape=jax.ShapeDtypeStruct((n_tok, CONV_WIDTH), BF16),
        scratch_shapes=[pltpu.VMEM((V7X_SUBLANES, nc), F32)],
        compiler_params=_params("arbitrary", "arbitrary"),
        name="conv_branch",
    )(x_bf, w_in_bf, w_in_bf, w_in_bf, conv_w)


def _attn_kernel(sinks_ref, q_ref, kc_ref, kp_ref, vc_ref, vp_ref, *refs):
    n_side = (len(refs) - 1) // 2
    o_ref = refs[n_side]
    for src, dst in zip(refs[:n_side], refs[n_side + 1:]):
        dst[...] = src[...].astype(BF16)
    n = pl.program_id(1)
    cols = Q_PER_KV * WINDOW
    kj = lax.broadcasted_iota(jnp.int32, (2 * WINDOW, cols), 0)
    qi = lax.broadcasted_iota(jnp.int32, (2 * WINDOW, cols), 1) & (WINDOW - 1)
    rel = WINDOW + qi - kj
    mask = (rel >= 0) & (rel < WINDOW) & ((kj >= WINDOW) | (n > 0))
    scale = HEAD_DIM ** -0.5
    for j in range(N_KV_HEADS):
        heads = [j * Q_PER_KV + g for g in range(Q_PER_KV)]
        qg = jnp.concatenate([q_ref[:, h * HEAD_DIM:(h + 1) * HEAD_DIM] for h in heads], axis=0)
        ksl = slice(j * HEAD_DIM, (j + 1) * HEAD_DIM)
        kw = jnp.concatenate([kp_ref[:, ksl], kc_ref[:, ksl]], axis=0)
        vw = jnp.concatenate([vp_ref[:, ksl], vc_ref[:, ksl]], axis=0)
        s = lax.dot_general(kw, qg * scale, (((1,), (1,)), ((), ())), preferred_element_type=F32)
        s = jnp.where(mask, s, -jnp.inf)
        sink = jnp.concatenate([jnp.full((1, WINDOW), sinks_ref[h], F32) for h in heads], axis=1)
        m = jnp.maximum(jnp.max(s, axis=0, keepdims=True), sink)
        p = jnp.exp(s - m)
        denom = jnp.sum(p, axis=0, keepdims=True) + jnp.exp(sink - m)
        probs = (p / denom).astype(BF16)
        o = lax.dot_general(probs, vw, (((0,), (0,)), ((), ())), preferred_element_type=F32)
        for g, h in enumerate(heads):
            o_ref[:, h * HEAD_DIM:(h + 1) * HEAD_DIM] = o[g * WINDOW:(g + 1) * WINDOW].astype(o_ref.dtype)


ATTN_SIDE_ROWS_MAX = 64


def _attention(qkv, sinks, batch, seq, weights):
    n_tok = qkv.shape[0]
    nb = seq // WINDOW
    cur = lambda b, n: b * nb + n
    prev = lambda b, n: b * nb + jnp.maximum(n - 1, 0)
    kcol, vcol = K_OFF // KV_WIDTH, V_OFF // KV_WIDTH
    rows = D_MODEL // (batch * nb)
    fuse = (rows * batch * nb == D_MODEL and rows % (2 * V7X_SUBLANES) == 0 and rows <= ATTN_SIDE_ROWS_MAX)
    side = weights if fuse else []
    wspec = lambda w: pl.BlockSpec((rows, w.shape[1]), lambda b, n: (cur(b, n), 0))
    outs = pl.pallas_call(
        _attn_kernel,
        grid=(batch, nb),
        in_specs=[pl.BlockSpec(memory_space=pltpu.SMEM),
                  pl.BlockSpec((WINDOW, ATTN_WIDTH), lambda b, n: (cur(b, n), 0)),
                  pl.BlockSpec((WINDOW, KV_WIDTH), lambda b, n: (cur(b, n), kcol)),
                  pl.BlockSpec((WINDOW, KV_WIDTH), lambda b, n: (prev(b, n), kcol)),
                  pl.BlockSpec((WINDOW, KV_WIDTH), lambda b, n: (cur(b, n), vcol)),
                  pl.BlockSpec((WINDOW, KV_WIDTH), lambda b, n: (prev(b, n), vcol))]
        + [wspec(w) for w in side],
        out_specs=[pl.BlockSpec((WINDOW, ATTN_WIDTH), lambda b, n: (cur(b, n), 0))] + [wspec(w) for w in side],
        out_shape=[jax.ShapeDtypeStruct((n_tok, ATTN_WIDTH), BF16)]
        + [jax.ShapeDtypeStruct(w.shape, BF16) for w in side],
        compiler_params=_params("arbitrary", "arbitrary"),
        name="swa_attention",
    )(sinks, qkv, qkv, qkv, qkv, qkv, *side)
    if not fuse:
        outs = list(outs) + [w.astype(BF16) for w in weights]
    return outs


def _layer_norm(r, g, b):
    mu = jnp.mean(r, axis=-1, keepdims=True)
    d = r - mu
    var = jnp.mean(d * d, axis=-1, keepdims=True)
    return d * lax.rsqrt(var + LN_EPS) * g + b


def _sigmoid(z):
    return 1.0 / (1.0 + jnp.exp(-z))


def _mix_kernel(xb_ref, xf_ref, a_ref, c_ref, wga_ref, wgc_ref, bga_ref, bgc_ref, wao_ref, wco_ref,
                wo_ref, g_ref, b_ref, hT_ref, acc_ref):
    j = pl.program_id(1)

    @pl.when(j == 0)
    def _():
        acc_ref[...] = jnp.zeros_like(acc_ref)

    xb = xb_ref[...]
    ga = _sigmoid(_dot(xb, wga_ref[...]) + bga_ref[...])
    gc = _sigmoid(_dot(xb, wgc_ref[...]) + bgc_ref[...])
    merged = ga * _dot(a_ref[...], wao_ref[...]) + gc * _dot(c_ref[...], wco_ref[...])
    acc_ref[...] += _dot(merged.astype(BF16), wo_ref[...])

    @pl.when(j == pl.num_programs(1) - 1)
    def _():
        h1 = _layer_norm(ALPHA * xf_ref[...] + acc_ref[...], g_ref[...], b_ref[...])
        hT_ref[...] = h1.T


def _mix_ln1(x_bf, x_f32, attn_o, conv_o, w_in_bf, b_gate, w_ao, w_co, w_o, ln_g, ln_b, tm, nc):
    n_tok = x_bf.shape[0]
    row = lambda i, j: (i, 0)
    gcol = G_OFF // nc
    return pl.pallas_call(
        _mix_kernel,
        grid=(n_tok // tm, D_MODEL // nc),
        in_specs=[pl.BlockSpec((tm, D_MODEL), row), pl.BlockSpec((tm, D_MODEL), row),
                  pl.BlockSpec((tm, ATTN_WIDTH), row), pl.BlockSpec((tm, CONV_WIDTH), row),
                  pl.BlockSpec((D_MODEL, nc), lambda i, j: (0, gcol + j)),
                  pl.BlockSpec((D_MODEL, nc), lambda i, j: (0, gcol + D_MODEL // nc + j)),
                  pl.BlockSpec((1, nc), lambda i, j: (0, j)),
                  pl.BlockSpec((1, nc), lambda i, j: (0, D_MODEL // nc + j)),
                  pl.BlockSpec((ATTN_WIDTH, nc), lambda i, j: (0, j)),
                  pl.BlockSpec((CONV_WIDTH, nc), lambda i, j: (0, j)),
                  pl.BlockSpec((nc, D_MODEL), lambda i, j: (j, 0)),
                  pl.BlockSpec((1, D_MODEL), lambda i, j: (0, 0)),
                  pl.BlockSpec((1, D_MODEL), lambda i, j: (0, 0))],
        out_specs=pl.BlockSpec((D_MODEL, tm), lambda i, j: (0, i)),
        out_shape=jax.ShapeDtypeStruct((D_MODEL, n_tok), F32),
        scratch_shapes=[pltpu.VMEM((tm, D_MODEL), F32)],
        compiler_params=_params("arbitrary", "arbitrary"),
        name="mix_ln1",
    )(x_bf, x_f32, attn_o, conv_o, w_in_bf, w_in_bf, b_gate, b_gate, w_ao, w_co, w_o, ln_g, ln_b)


def _merge_exchange(n):
    pairs = []
    t = max(1, math.ceil(math.log2(n)))
    p = 1 << (t - 1)
    while p > 0:
        q, r, d = 1 << (t - 1), 0, p
        while d > 0:
            pairs += [(i, i + d) for i in range(n - d) if (i & p) == r]
            d, q, r = q - p, q >> 1, p
        p >>= 1
    return pairs


def _top_values(groups, k):
    v = list(groups)
    for i, j in _merge_exchange(len(v)):
        v[i], v[j] = jnp.maximum(v[i], v[j]), jnp.minimum(v[i], v[j])
    v.append(jnp.full(v[0].shape, -jnp.inf, F32))
    out = []
    for r in range(k):
        m = jnp.max(v[0], axis=0, keepdims=True)
        out.append(m)
        pop = v[0] == m
        for d in range(min(len(v) - 1, k - 1 - r)):
            v[d] = jnp.where(pop, v[d + 1], v[d])
    return out


def _row_groups(x):
    return [x[V7X_SUBLANES * g:V7X_SUBLANES * (g + 1)] for g in range(x.shape[0] // V7X_SUBLANES)]


def _candidate_groups(a, b):
    sub = lax.broadcasted_iota(jnp.int32, (V7X_SUBLANES, a[0].shape[1]), 0)
    pad = (jnp.full_like(a[0], -jnp.inf), jnp.zeros_like(b[0]))
    rows = [(a[r1], b[r2]) for r1, r2 in _CAND] + [pad] * (N_CAND_ROWS - len(_CAND))
    sums, prods = [], []
    for g in range(N_CAND_ROWS // V7X_SUBLANES):
        grp = rows[V7X_SUBLANES * g:V7X_SUBLANES * (g + 1)]
        ca = jnp.broadcast_to(grp[0][0], sub.shape)
        cb = jnp.broadcast_to(grp[0][1], sub.shape)
        for j in range(1, V7X_SUBLANES):
            if grp[j][0] is not grp[j - 1][0]:
                ca = jnp.where(sub >= j, grp[j][0], ca)
            if grp[j][1] is not grp[j - 1][1]:
                cb = jnp.where(sub >= j, grp[j][1], cb)
        sums.append(ca + cb)
        prods.append(jnp.exp(ca - a[0]) * jnp.exp(cb - b[0]))
    return sums, prods


def _count_above(rows, x):
    n = jnp.zeros(x.shape, F32)
    for r, row in enumerate(rows):
        n = jnp.where(row > x, float(r + 1), n)
    return n


def _route_kernel(hT_ref, wqT_ref, sk_ref, *refs):
    if len(refs) == 4:
        n1_ref, e1_ref, r2_ref, e2_ref = refs
    else:
        u_ref, v_ref, n1_ref, e1_ref, r2_ref, e2_ref, ub_ref, vT_ref = refs
        ub_ref[...] = u_ref[...].astype(BF16)
        vT_ref[...] = v_ref[...].T.astype(BF16)
    qT = _dot(wqT_ref[...], hT_ref[...].astype(BF16)).astype(BF16)

    for h in range(PEER_HEADS):
        q1 = qT[h * PEER_QDIM:h * PEER_QDIM + PEER_HALF]
        q2 = qT[h * PEER_QDIM + PEER_HALF:(h + 1) * PEER_QDIM]
        s1 = _dot(sk_ref[h, 0], q1)
        s2 = _dot(sk_ref[h, 1], q2)
        a = _top_values(_row_groups(s1), N_RANKS)
        b = _top_values(_row_groups(s2), N_RANKS)
        csum, cprod = _candidate_groups(a, b)
        c = _top_values(csum, N_RANKS)
        thr = 0.5 * (c[PEER_TOPK - 1] + c[PEER_TOPK])
        picked = [jnp.where(s >= c[PEER_TOPK - 1], p, 0.0) for s, p in zip(csum, cprod)]
        denom = jnp.sum(functools.reduce(jnp.add, picked), axis=0, keepdims=True)
        split = (PEER_N_KEYS // V7X_SUBLANES, V7X_SUBLANES, s1.shape[1])
        n1_ref[h] = _count_above(b[:PEER_TOPK], thr - s1).reshape(split)
        e1_ref[h] = (jnp.exp(s1 - a[0]) * (1.0 / denom)).reshape(split)
        r2_ref[h] = _count_above(b[:PEER_TOPK], s2).astype(BF16)
        e2_ref[h] = jnp.exp(s2 - b[0]).astype(BF16)


PEER_TABLE_ROWS_MAX = 512


def _peer_route(hT, wqT_bf, sub_keys_bf, u_tab, v_tab, tm):
    n_tok = hT.shape[1]
    steps = n_tok // tm
    groups = PEER_N_KEYS // V7X_SUBLANES
    out4 = jax.ShapeDtypeStruct((PEER_HEADS, groups, V7X_SUBLANES, n_tok), F32)
    out3 = jax.ShapeDtypeStruct((PEER_HEADS, PEER_N_KEYS, n_tok), BF16)
    spec4 = pl.BlockSpec((PEER_HEADS, groups, V7X_SUBLANES, tm), lambda i: (0, 0, 0, i))
    spec3 = pl.BlockSpec((PEER_HEADS, PEER_N_KEYS, tm), lambda i: (0, 0, i))
    in_specs = [pl.BlockSpec((D_MODEL, tm), lambda i: (0, i)),
                pl.BlockSpec((PEER_HEADS * PEER_QDIM, D_MODEL), lambda i: (0, 0), pipeline_mode=pl.Buffered(1)),
                pl.BlockSpec((PEER_HEADS, 2, PEER_N_KEYS, PEER_HALF), lambda i: (0, 0, 0, 0))]
    out_specs, out_shape, args = [spec4, spec4, spec3, spec3], [out4, out4, out3, out3], [hT, wqT_bf, sub_keys_bf]
    rows = PEER_N_EXPERTS // steps
    fuse_tables = rows * steps == PEER_N_EXPERTS and rows % V7X_LANES == 0 and rows <= PEER_TABLE_ROWS_MAX
    if fuse_tables:
        in_specs += [pl.BlockSpec((rows, D_MODEL), lambda i: (i, 0))] * 2
        out_specs += [pl.BlockSpec((rows, D_MODEL), lambda i: (i, 0)), pl.BlockSpec((D_MODEL, rows), lambda i: (0, i))]
        out_shape += [jax.ShapeDtypeStruct((PEER_N_EXPERTS, D_MODEL), BF16),
                      jax.ShapeDtypeStruct((D_MODEL, PEER_N_EXPERTS), BF16)]
        args += [u_tab, v_tab]
    outs = pl.pallas_call(
        _route_kernel,
        grid=(steps,),
        in_specs=in_specs,
        out_specs=out_specs,
        out_shape=out_shape,
        compiler_params=_params("arbitrary"),
        name="peer_route",
    )(*args)
    if not fuse_tables:
        outs = list(outs) + [u_tab.astype(BF16), v_tab.T.astype(BF16)]
    return outs


def _gelu(x):
    return 0.5 * x * (1.0 + lax.erf(x * (2.0 ** -0.5)))


def _rows_bf16(row, n_rows):
    packed_rows = 2 * V7X_SUBLANES
    tile = jnp.broadcast_to(row, (packed_rows, row.shape[1])).astype(BF16)
    return jnp.concatenate([tile] * (n_rows // packed_rows), axis=0)


def _peer_kernel(hT_ref, u_ref, vT_ref, n1_ref, e1_ref, r2_ref, e2_ref, g_ref, b_ref, o_ref,
                 hb_ref, acc_ref):
    j = pl.program_id(1)

    @pl.when(j == 0)
    def _():
        hb_ref[...] = hT_ref[...].astype(BF16)
        acc_ref[...] = jnp.zeros_like(acc_ref)

    slabs = []
    for i1 in range(u_ref.shape[0] // PEER_N_KEYS):
        rows = slice(i1 * PEER_N_KEYS, (i1 + 1) * PEER_N_KEYS)
        act = _gelu(_dot(u_ref[rows, :], hb_ref[...])).astype(BF16)
        grp, sub = divmod(i1, V7X_SUBLANES)
        w = None
        for h in range(PEER_HEADS):
            sel = r2_ref[h] < _rows_bf16(n1_ref[h, grp, sub:sub + 1, :], PEER_N_KEYS)
            wh = jnp.where(sel, _rows_bf16(e1_ref[h, grp, sub:sub + 1, :], PEER_N_KEYS), 0.0) * e2_ref[h]
            w = wh if w is None else w + wh
        slabs.append(act * w)
    acc_ref[...] += _dot(vT_ref[...], jnp.concatenate(slabs, axis=0))

    @pl.when(j == pl.num_programs(1) - 1)
    def _():
        r = (ALPHA * hT_ref[...] + acc_ref[...]).T
        o_ref[...] = _layer_norm(r, g_ref[...], b_ref[...])


def _peer_experts(hT, u_bf, vT_bf, n1, e1, r2, e2, ln_g, ln_b, tm, te):
    n_tok = hT.shape[1]
    groups = te // (PEER_N_KEYS * V7X_SUBLANES)
    assert groups * PEER_N_KEYS * V7X_SUBLANES == te
    tok = lambda i, j: (0, 0, i)
    blk = lambda i, j: (0, j, 0, i)
    return pl.pallas_call(
        _peer_kernel,
        grid=(n_tok // tm, PEER_N_EXPERTS // te),
        in_specs=[pl.BlockSpec((D_MODEL, tm), lambda i, j: (0, i)),
                  pl.BlockSpec((te, D_MODEL), lambda i, j: (j, 0)),
                  pl.BlockSpec((D_MODEL, te), lambda i, j: (0, j)),
                  pl.BlockSpec((PEER_HEADS, groups, V7X_SUBLANES, tm), blk),
                  pl.BlockSpec((PEER_HEADS, groups, V7X_SUBLANES, tm), blk),
                  pl.BlockSpec((PEER_HEADS, PEER_N_KEYS, tm), tok),
                  pl.BlockSpec((PEER_HEADS, PEER_N_KEYS, tm), tok),
                  pl.BlockSpec((1, D_MODEL), lambda i, j: (0, 0)),
                  pl.BlockSpec((1, D_MODEL), lambda i, j: (0, 0))],
        out_specs=pl.BlockSpec((tm, D_MODEL), lambda i, j: (i, 0)),
        out_shape=jax.ShapeDtypeStruct((n_tok, D_MODEL), F32),
        scratch_shapes=[pltpu.VMEM((D_MODEL, tm), BF16), pltpu.VMEM((D_MODEL, tm), F32)],
        compiler_params=_params("arbitrary", "arbitrary"),
        name="peer_experts",
    )(hT, u_bf, vT_bf, n1, e1, r2, e2, ln_g, ln_b)


def _layer(h, batch, seq, w_in, b_gate, sinks, conv_w, w_ao, w_co, w_o, ln1_g, ln1_b,
           w_q, sub_keys, u_tab, v_tab, ln2_g, ln2_b):
    n_tok = h.shape[0]
    t = _tiles(n_tok, seq)
    row = lambda v: v.reshape(1, -1)
    h_bf = h.astype(BF16)
    qkv = _qkv_proj(h_bf, w_in, t["proj"], t["col"])
    attn_o, w_in_bf, w_ao_bf, w_co_bf, w_o_bf = _attention(qkv, sinks, batch, seq, [w_in, w_ao, w_co, w_o])
    conv_o = _conv_branch(h_bf, w_in_bf, conv_w, seq, t["proj"], t["col"])
    hT = _mix_ln1(h_bf, h, attn_o, conv_o, w_in_bf, row(b_gate), w_ao_bf, w_co_bf, w_o_bf,
                  row(ln1_g), row(ln1_b), t["mix"], t["mix_col"])
    n1, e1, r2, e2, u_bf, vT_bf = _peer_route(hT, w_q.T.astype(BF16), sub_keys.astype(BF16), u_tab, v_tab,
                                              t["route"])
    return _peer_experts(hT, u_bf, vT_bf, n1, e1, r2, e2, row(ln2_g), row(ln2_b), t["peer"], t["experts"])


def kernel(x, w_in, b_gate, sinks, conv_w, w_attn_out, w_conv_out, w_o, ln1_g, ln1_b, peer_w_q,
           peer_sub_keys, peer_u, peer_v, ln2_g, ln2_b):
    batch, seq, d_model = x.shape
    assert d_model == D_MODEL and seq % WINDOW == 0
    h = x.reshape(batch * seq, d_model)
    for l in range(w_in.shape[0]):
        h = _layer(h, batch, seq, w_in[l], b_gate[l], sinks[l], conv_w[l], w_attn_out[l],
                   w_conv_out[l], w_o[l], ln1_g[l], ln1_b[l], peer_w_q[l], peer_sub_keys[l],
                   peer_u[l], peer_v[l], ln2_g[l], ln2_b[l])
    return h.reshape(batch, seq, d_model)
```
